```python
import math
import jax, jax.numpy as jnp
from jax import lax
import numpy as np

D_MODEL = 1024
BATCH = 32
SEQ = 2048
DEPTH = 1

SB_HEADS = 16
SB_HEAD_DIM = 64
SB_WIDTH = SB_HEADS * SB_HEAD_DIM
CONV_CHANNELS = D_MODEL
CONV_WIDTH = 31
D_FF = 2816
Q_BLOCK = 128
N_SUBLAYERS = 3
N_MOD = 3
MACARON_WEIGHT = 0.5
DEEPNORM_ALPHA = (2.0 * DEPTH) ** 0.25
DEEPNORM_BETA = (8.0 * DEPTH) ** -0.25
LN_EPS = 1e-5
IN_SPLITS = (SB_WIDTH, SB_WIDTH, SB_WIDTH, CONV_CHANNELS, CONV_CHANNELS, D_MODEL, D_MODEL)
IN_WIDTH = sum(IN_SPLITS)

kernel_name = "hybrid_stickbreak_conformer_macaron_deepnorm_adaln"


def layer_norm(x, g, b):
    xf = x.astype(jnp.float32)
    mu = jnp.mean(xf, axis=-1, keepdims=True)
    var = jnp.mean(jnp.square(xf - mu), axis=-1, keepdims=True)
    y = (xf - mu) * lax.rsqrt(var + LN_EPS) * g.astype(jnp.float32) + b.astype(jnp.float32)
    return y.astype(x.dtype)


def swiglu(u, w_gu, w_down):
    a, g = jnp.split(u @ w_gu, 2, axis=-1)
    return (jax.nn.silu(a) * g) @ w_down


def stick_breaking_attention(q, k, v):
    seq = q.shape[2]
    scale = 1.0 / math.sqrt(q.shape[-1])
    outs = []
    for blk in range(seq // Q_BLOCK):
        start = blk * Q_BLOCK
        end = start + Q_BLOCK
        qb = q[:, :, start:end]
        kb = k[:, :, :end]
        vb = v[:, :, :end]
        z = jnp.einsum('bhqd,bhkd->bhqk', qb, kb).astype(jnp.float32) * scale
        t_idx = start + jnp.arange(Q_BLOCK)[:, None]
        s_idx = jnp.arange(end)[None, :]
        mask = s_idx < t_idx
        log_keep = jnp.where(mask, jax.nn.log_sigmoid(-z), 0.0)
        between = lax.cumsum(log_keep, axis=3, reverse=True) - log_keep
        log_w = jax.nn.log_sigmoid(z) + between
        w = jnp.where(mask, jnp.exp(log_w), 0.0)
        outs.append(jnp.einsum('bhqk,bhkd->bhqd', w.astype(vb.dtype), vb))
    return jnp.concatenate(outs, axis=2)


def conformer_conv(a, b, w_dw, b_dw, g, beta):
    h = a * jax.nn.sigmoid(b)
    h = lax.conv_general_dilated(
        h, w_dw[:, None, :].astype(h.dtype), window_strides=(1,),
        padding=[(CONV_WIDTH - 1, 0)], dimension_numbers=('NWC', 'WIO', 'NWC'),
        feature_group_count=CONV_CHANNELS) + b_dw
    return jax.nn.silu(layer_norm(h, g, beta))


def setup_inputs(seed: int = 0) -> dict:
    key = jax.random.key(seed)
    ks = jax.random.split(key, 32)
    L, D, C = DEPTH, D_MODEL, CONV_CHANNELS

    def nrm(k, shape, std):
        return jax.random.normal(k, shape, jnp.float32) * std

    x = nrm(ks[0], (BATCH, SEQ, D), 1.0)
    c = nrm(ks[1], (BATCH, D), 1.0)
    w_ada = nrm(ks[2], (L, D, N_SUBLAYERS * N_MOD * D), 0.5 * D ** -0.5)
    b_ada = nrm(ks[3], (L, N_SUBLAYERS * N_MOD * D), 0.02)
    ffn1_w_gu = nrm(ks[4], (L, D, 2 * D_FF), D ** -0.5)
    ffn1_w_down = nrm(ks[5], (L, D_FF, D), D_FF ** -0.5 * DEEPNORM_BETA)
    ln1_g = 1.0 + nrm(ks[6], (L, D), 0.02)
    ln1_b = nrm(ks[7], (L, D), 0.02)
    w_qk = nrm(ks[8], (L, D, 2 * SB_WIDTH), D ** -0.5)
    w_v = nrm(ks[9], (L, D, SB_WIDTH), D ** -0.5 * DEEPNORM_BETA)
    w_rest = nrm(ks[10], (L, D, 2 * C + 2 * D), D ** -0.5)
    w_in = jnp.concatenate([w_qk, w_v, w_rest], axis=-1)
    w_sb_out = nrm(ks[11], (L, SB_WIDTH, D), SB_WIDTH ** -0.5 * DEEPNORM_BETA)
    conv_w = nrm(ks[12], (L, CONV_WIDTH, C), CONV_WIDTH ** -0.5)
    conv_b = nrm(ks[13], (L, C), 0.02)
    conv_ln_g = 1.0 + nrm(ks[14], (L, C), 0.02)
    conv_ln_b = nrm(ks[15], (L, C), 0.02)
    w_conv_out = nrm(ks[16], (L, C, D), C ** -0.5 * DEEPNORM_BETA)
    w_out = nrm(ks[17], (L, D, D), D ** -0.5 * DEEPNORM_BETA)
    ln2_g = 1.0 + nrm(ks[18], (L, D), 0.02)
    ln2_b = nrm(ks[19], (L, D), 0.02)
    ffn2_w_gu = nrm(ks[20], (L, D, 2 * D_FF), D ** -0.5)
    ffn2_w_down = nrm(ks[21], (L, D_FF, D), D_FF ** -0.5 * DEEPNORM_BETA)
    ln3_g = 1.0 + nrm(ks[22], (L, D), 0.02)
    ln3_b = nrm(ks[23], (L, D), 0.02)
    return {"x": x, "c": c, "w_ada": w_ada, "b_ada": b_ada,
            "ffn1_w_gu": ffn1_w_gu, "ffn1_w_down": ffn1_w_down, "ln1_g": ln1_g, "ln1_b": ln1_b,
            "w_in": w_in, "w_sb_out": w_sb_out, "conv_w": conv_w, "conv_b": conv_b,
            "conv_ln_g": conv_ln_g, "conv_ln_b": conv_ln_b, "w_conv_out": w_conv_out,
            "w_out": w_out, "ln2_g": ln2_g, "ln2_b": ln2_b,
            "ffn2_w_gu": ffn2_w_gu, "ffn2_w_down": ffn2_w_down, "ln3_g": ln3_g, "ln3_b": ln3_b}


def reference(x, c, w_ada, b_ada, ffn1_w_gu, ffn1_w_down, ln1_g, ln1_b, w_in, w_sb_out,
              conv_w, conv_b, conv_ln_g, conv_ln_b, w_conv_out, w_out, ln2_g, ln2_b,
              ffn2_w_gu, ffn2_w_down, ln3_g, ln3_b):
    bsz, seq, _ = x.shape
    split_idx = list(np.cumsum(IN_SPLITS)[:-1])
    for l in range(DEPTH):
        mod = (jax.nn.silu(c) @ w_ada[l] + b_ada[l]).reshape(bsz, N_SUBLAYERS * N_MOD, 1, D_MODEL)
        sh1, sc1, g1, sh2, sc2, g2, sh3, sc3, g3 = [mod[:, i] for i in range(N_SUBLAYERS * N_MOD)]

        u = x * (1.0 + sc1) + sh1
        x = layer_norm(DEEPNORM_ALPHA * x + g1 * (MACARON_WEIGHT * swiglu(u, ffn1_w_gu[l], ffn1_w_down[l])),
                       ln1_g[l], ln1_b[l])

        u = x * (1.0 + sc2) + sh2
        q, k, v, glu_a, glu_b, gate_a, gate_b = jnp.split(u @ w_in[l], split_idx, axis=-1)
        heads = lambda t: t.reshape(bsz, seq, SB_HEADS, SB_HEAD_DIM).transpose(0, 2, 1, 3)
        y_sb = stick_breaking_attention(heads(q), heads(k), heads(v))
        y_sb = y_sb.transpose(0, 2, 1, 3).reshape(bsz, seq, SB_WIDTH) @ w_sb_out[l]
        y_conv = conformer_conv(glu_a, glu_b, conv_w[l], conv_b[l], conv_ln_g[l], conv_ln_b[l]) @ w_conv_out[l]
        merged = jax.nn.sigmoid(gate_a) * y_sb + jax.nn.sigmoid(gate_b) * y_conv
        x = layer_norm(DEEPNORM_ALPHA * x + g2 * (merged @ w_out[l]), ln2_g[l], ln2_b[l])

        u = x * (1.0 + sc3) + sh3
        x = layer_norm(DEEPNORM_ALPHA * x + g3 * (MACARON_WEIGHT * swiglu(u, ffn2_w_gu[l], ffn2_w_down[l])),
                       ln3_g[l], ln3_b[l])
    return x
```

```python
import functools
import math

import jax
import jax.numpy as jnp
from jax import lax
from jax.experimental import pallas as pl
from jax.experimental.pallas import tpu as pltpu

F32 = jnp.float32
BF16 = jnp.bfloat16

D_MODEL = 1024
SB_HEADS = 16
SB_HEAD_DIM = 64
D_FF = 2816
CONV_WIDTH = 31
N_MOD_ROWS = 9
MACARON_WEIGHT = 0.5
DEEPNORM_ALPHA = 2.0 ** 0.25
LN_EPS = 1e-5

ATT_BLK = 256
ATT_TOTAL_ROWS = 16
HEAD_PAIR = 2 * SB_HEAD_DIM
CONV_HALO = 32

FFN_TM = 512
FFN_FW = 256
INPROJ_TM = 512
MIX_TM = 256
VMEM_LIMIT = 56 * 1024 * 1024

_NT = (((1,), (1,)), ((), ()))


def _layer_norm(r, g, b):
    mu = jnp.mean(r, axis=-1, keepdims=True)
    d = r - mu
    var = jnp.mean(d * d, axis=-1, keepdims=True)
    return d * lax.rsqrt(var + LN_EPS) * g + b


def _const_spec(shape):
    nd = len(shape)
    return pl.BlockSpec(shape, lambda *_: (0,) * nd, pipeline_mode=pl.Buffered(1))


def _mod_kernel(c_ref, w_ref, b_ref, o_ref):
    c = c_ref[...]
    s = c * jax.nn.sigmoid(c)
    o_ref[...] = jnp.dot(s, w_ref[...], preferred_element_type=F32,
                         precision=lax.Precision.HIGHEST) + b_ref[...]


def _modulation(c, w_ada, b_ada):
    bsz, d = c.shape
    n = w_ada.shape[1]
    tn = 1024
    return pl.pallas_call(
        _mod_kernel,
        grid=(n // tn,),
        in_specs=[pl.BlockSpec((bsz, d), lambda j: (0, 0)),
                  pl.BlockSpec((d, tn), lambda j: (0, j)),
                  pl.BlockSpec((1, tn), lambda j: (0, j))],
        out_specs=pl.BlockSpec((bsz, tn), lambda j: (0, j)),
        out_shape=jax.ShapeDtypeStruct((bsz, n), F32),
        compiler_params=pltpu.CompilerParams(dimension_semantics=("arbitrary",)),
        name="mod",
    )(c, w_ada, b_ada.reshape(1, n))


def _ffn_kernel(x_ref, mod_ref, wgu_ref, wd_ref, lng_ref, lnb_ref, o_ref, h_ref, *, sub):
    x = x_ref[...]
    sh = mod_ref[3 * sub:3 * sub + 1, :]
    sc = mod_ref[3 * sub + 1:3 * sub + 2, :]
    gate = mod_ref[3 * sub + 2:3 * sub + 3, :]
    u = (x * (1.0 + sc) + sh).astype(BF16)
    for c in range(D_FF // FFN_FW):
        lo = c * FFN_FW
        a = jnp.dot(u, wgu_ref[:, lo:lo + FFN_FW], preferred_element_type=F32)
        g = jnp.dot(u, wgu_ref[:, D_FF + lo:D_FF + lo + FFN_FW], preferred_element_type=F32)
        h_ref[:, lo:lo + FFN_FW] = (a * jax.nn.sigmoid(a) * g).astype(BF16)
    y = jnp.dot(h_ref[...], wd_ref[...], preferred_element_type=F32)
    r = DEEPNORM_ALPHA * x + gate * (MACARON_WEIGHT * y)
    o_ref[...] = _layer_norm(r, lng_ref[...], lnb_ref[...])


def _ffn(x, mod, w_gu, w_down, ln_g, ln_b, sub):
    bsz, seq, d = x.shape
    tm = FFN_TM
    return pl.pallas_call(
        functools.partial(_ffn_kernel, sub=sub),
        grid=(bsz, seq // tm),
        in_specs=[pl.BlockSpec((None, tm, d), lambda b, t: (b, t, 0)),
                  pl.BlockSpec((None, N_MOD_ROWS, d), lambda b, t: (b, 0, 0)),
                  _const_spec((d, 2 * D_FF)),
                  _const_spec((D_FF, d)),
                  _const_spec((1, d)),
                  _const_spec((1, d))],
        out_specs=pl.BlockSpec((None, tm, d), lambda b, t: (b, t, 0)),
        out_shape=jax.ShapeDtypeStruct((bsz, seq, d), F32),
        scratch_shapes=[pltpu.VMEM((tm, D_FF), BF16)],
        compiler_params=pltpu.CompilerParams(
            dimension_semantics=("arbitrary", "arbitrary"), vmem_limit_bytes=VMEM_LIMIT),
        name=f"ffn{sub}",
    )(x, mod, w_gu.astype(BF16), w_down.astype(BF16), ln_g.reshape(1, d), ln_b.reshape(1, d))


def _inproj_kernel(x_ref, mod_ref, wqt_ref, wk_ref, wvt_ref, wr_ref,
                   qt_ref, k_ref, vt_ref, h_ref, ga_ref, gb_ref):
    x = x_ref[...]
    sh = mod_ref[3:4, :]
    sc = mod_ref[4:5, :]
    u = (x * (1.0 + sc) + sh).astype(BF16)
    k_ref[...] = jnp.dot(u, wk_ref[...], preferred_element_type=F32).astype(BF16)
    qt = lax.dot_general(wqt_ref[...], u, _NT, preferred_element_type=F32)
    vt = lax.dot_general(wvt_ref[...], u, _NT, preferred_element_type=F32)
    for t in range(INPROJ_TM // ATT_BLK):
        qt_ref[t] = qt[:, t * ATT_BLK:(t + 1) * ATT_BLK].astype(BF16)
        vt_ref[t] = vt[:, t * ATT_BLK:(t + 1) * ATT_BLK].astype(BF16)
    d = D_MODEL
    a = jnp.dot(u, wr_ref[:, 0:d], preferred_element_type=F32)
    b = jnp.dot(u, wr_ref[:, d:2 * d], preferred_element_type=F32)
    h_ref[...] = a * jax.nn.sigmoid(b)
    ga_ref[...] = jax.nn.sigmoid(
        jnp.dot(u, wr_ref[:, 2 * d:3 * d], preferred_element_type=F32)).astype(BF16)
    gb_ref[...] = jax.nn.sigmoid(
        jnp.dot(u, wr_ref[:, 3 * d:4 * d], preferred_element_type=F32)).astype(BF16)


def _inproj(x, mod, w_in):
    bsz, seq, d = x.shape
    tm = INPROJ_TM
    nb = tm // ATT_BLK
    sbw = SB_HEADS * SB_HEAD_DIM
    scale = 1.0 / math.sqrt(SB_HEAD_DIM)
    wqt = (w_in[:, 0:sbw] * scale).T.astype(BF16)
    wk = w_in[:, sbw:2 * sbw].astype(BF16)
    wvt = w_in[:, 2 * sbw:3 * sbw].T.astype(BF16)
    wr = w_in[:, 3 * sbw:].astype(BF16)
    tok = pl.BlockSpec((None, tm, d), lambda b, t: (b, t, 0))
    blk_t = pl.BlockSpec((None, nb, sbw, ATT_BLK), lambda b, t: (b, t, 0, 0))
    return pl.pallas_call(
        _inproj_kernel,
        grid=(bsz, seq // tm),
        in_specs=[tok,
                  pl.BlockSpec((None, N_MOD_ROWS, d), lambda b, t: (b, 0, 0)),
                  _const_spec((sbw, d)), _const_spec((d, sbw)), _const_spec((sbw, d)),
                  _const_spec((d, 4 * d))],
        out_specs=[blk_t, tok, blk_t, tok, tok, tok],
        out_shape=[jax.ShapeDtypeStruct((bsz, seq // ATT_BLK, sbw, ATT_BLK), BF16),
                   jax.ShapeDtypeStruct((bsz, seq, sbw), BF16),
                   jax.ShapeDtypeStruct((bsz, seq // ATT_BLK, sbw, ATT_BLK), BF16),
                   jax.ShapeDtypeStruct((bsz, seq, d), F32),
                   jax.ShapeDtypeStruct((bsz, seq, d), BF16),
                   jax.ShapeDtypeStruct((bsz, seq, d), BF16)],
        compiler_params=pltpu.CompilerParams(
            dimension_semantics=("arbitrary", "arbitrary"), vmem_limit_bytes=VMEM_LIMIT),
        name="inproj",
    )(x, mod, wqt, wk, wvt, wr)


def _attn_kernel(qt_ref, k_ref, vt_ref, ut_ref, o_ref):
    blk = ATT_BLK
    n_blk = qt_ref.shape[0]
    dim_row = lax.broadcasted_iota(jnp.int32, (HEAD_PAIR, blk), 0)
    key_pos = lax.broadcasted_iota(jnp.int32, (blk, blk), 0)
    qry_pos = lax.broadcasted_iota(jnp.int32, (blk, blk), 1)
    causal = key_pos < qry_pos
    ut = ut_ref[...]

    def block(qh, head, j, masked):
        kj = k_ref[pl.ds(pl.multiple_of(j * blk, blk), blk), :]
        zt = jnp.dot(kj, qh, preferred_element_type=F32)
        sp = jnp.log(1.0 + jnp.exp(-jnp.abs(zt)))
        log_beta = jnp.minimum(zt, 0.0) - sp
        log_keep = log_beta - zt
        if masked:
            log_keep = jnp.where(causal, log_keep, 0.0)
        hi = log_keep.astype(BF16)
        lo = (log_keep - hi.astype(F32)).astype(BF16)
        cs = (jnp.dot(ut, hi, preferred_element_type=F32)
              + jnp.dot(ut, lo, preferred_element_type=F32))
        w = jnp.exp(log_beta + cs[0:blk])
        if masked:
            w = jnp.where(causal, w, 0.0)
        vj = vt_ref[j, head * SB_HEAD_DIM:(head + 1) * SB_HEAD_DIM, :]
        pv = jnp.dot(vj, w.astype(BF16), preferred_element_type=F32)
        return pv, cs[blk:blk + 8]

    def query_block(i, carry_unused):
        qt = qt_ref[i]
        heads = []
        for head in range(2):
            own = (dim_row >= SB_HEAD_DIM) if head else (dim_row < SB_HEAD_DIM)
            qh = jnp.where(own, qt, jnp.zeros_like(qt))
            acc, later = block(qh, head, i, True)

            def earlier_block(n, state, qh=qh, head=head):
                acc, later = state
                pv, total = block(qh, head, i - 1 - n, False)
                return acc + jnp.exp(later[0:1]) * pv, later + total

            acc, later = lax.fori_loop(0, i, earlier_block, (acc, later))
            heads.append(acc)
        out_t = jnp.concatenate(heads, axis=0)
        o_ref[pl.ds(pl.multiple_of(i * blk, blk), blk), :] = out_t.T.astype(BF16)
        return carry_unused

    lax.fori_loop(0, n_blk, query_block, 0)


def _attention(qt, k, vt):
    bsz, n_blk, sbw, blk = qt.shape
    seq = k.shape[1]
    ut = jnp.concatenate(
        [jnp.triu(jnp.ones((blk, blk), F32), k=1), jnp.ones((ATT_TOTAL_ROWS, blk), F32)],
        axis=0).astype(BF16)
    n_pairs = sbw // HEAD_PAIR
    blk_t = pl.BlockSpec((None, n_blk, HEAD_PAIR, blk), lambda b, p: (b, 0, p, 0))
    tok = pl.BlockSpec((None, seq, HEAD_PAIR), lambda b, p: (b, 0, p))
    return pl.pallas_call(
        _attn_kernel,
        grid=(bsz, n_pairs),
        in_specs=[blk_t, tok, blk_t, _const_spec((blk + ATT_TOTAL_ROWS, blk))],
        out_specs=tok,
        out_shape=jax.ShapeDtypeStruct((bsz, seq, sbw), BF16),
        compiler_params=pltpu.CompilerParams(
            dimension_semantics=("arbitrary", "arbitrary"), vmem_limit_bytes=VMEM_LIMIT),
        name="attn",
    )(qt, k, vt, ut)


def _mix_kernel(x_ref, mod_ref, attn_ref, h_ref, halo_ref, ga_ref, gb_ref,
                cw_ref, cb_ref, clg_ref, clb_ref, wsb_ref, wco_ref, wo_ref, lng_ref, lnb_ref,
                o_ref, hext_ref):
    tm = x_ref.shape[0]
    first = pl.program_id(1) == 0
    halo = halo_ref[...]
    hext_ref[0:CONV_HALO, :] = jnp.where(first, jnp.zeros_like(halo), halo)
    hext_ref[CONV_HALO:, :] = h_ref[...]
    base = CONV_HALO - (CONV_WIDTH - 1)
    conv = hext_ref[pl.ds(base, tm), :] * cw_ref[0:1, :] + cb_ref[...]
    for j in range(1, CONV_WIDTH):
        conv = conv + hext_ref[pl.ds(base + j, tm), :] * cw_ref[j:j + 1, :]
    cn = _layer_norm(conv, clg_ref[...], clb_ref[...])
    act = (cn * jax.nn.sigmoid(cn)).astype(BF16)
    y_conv = jnp.dot(act, wco_ref[...], preferred_element_type=F32)
    y_sb = jnp.dot(attn_ref[...], wsb_ref[...], preferred_element_type=F32)
    merged = ga_ref[...].astype(F32) * y_sb + gb_ref[...].astype(F32) * y_conv
    o = jnp.dot(merged.astype(BF16), wo_ref[...], preferred_element_type=F32)
    x = x_ref[...]
    gate = mod_ref[5:6, :]
    r = DEEPNORM_ALPHA * x + gate * o
    o_ref[...] = _layer_norm(r, lng_ref[...], lnb_ref[...])


def _mix(x, mod, attn, h, ga, gb, conv_w, conv_b, conv_ln_g, conv_ln_b,
         w_sb_out, w_conv_out, w_out, ln_g, ln_b):
    bsz, seq, d = x.shape
    tm = MIX_TM
    halo_per_tile = tm // CONV_HALO
    tok = pl.BlockSpec((None, tm, d), lambda b, t: (b, t, 0))
    halo = pl.BlockSpec((None, CONV_HALO, d),
                        lambda b, t: (b, jnp.maximum(t * halo_per_tile - 1, 0), 0))
    vec = _const_spec((1, d))
    return pl.pallas_call(
        _mix_kernel,
        grid=(bsz, seq // tm),
        in_specs=[tok,
                  pl.BlockSpec((None, N_MOD_ROWS, d), lambda b, t: (b, 0, 0)),
                  tok, tok, halo, tok, tok,
                  _const_spec((CONV_WIDTH, d)), vec, vec, vec,
                  _const_spec((d, d)), _const_spec((d, d)), _const_spec((d, d)), vec, vec],
        out_specs=tok,
        out_shape=jax.ShapeDtypeStruct((bsz, seq, d), F32),
        scratch_shapes=[pltpu.VMEM((tm + CONV_HALO, d), F32)],
        compiler_params=pltpu.CompilerParams(
            dimension_semantics=("arbitrary", "arbitrary"), vmem_limit_bytes=VMEM_LIMIT),
        name="mix",
    )(x, mod, attn, h, h, ga, gb, conv_w, conv_b.reshape(1, d), conv_ln_g.reshape(1, d),
      conv_ln_b.reshape(1, d), w_sb_out.astype(BF16), w_conv_out.astype(BF16), w_out.astype(BF16),
      ln_g.reshape(1, d), ln_b.reshape(1, d))


def kernel(x, c, w_ada, b_ada, ffn1_w_gu, ffn1_w_down, ln1_g, ln1_b, w_in, w_sb_out, conv_w, conv_b,
           conv_ln_g, conv_ln_b, w_conv_out, w_out, ln2_g, ln2_b, ffn2_w_gu, ffn2_w_down, ln3_g, ln3_b):
    bsz, seq, d = x.shape
    depth = w_ada.shape[0]
    for l in range(depth):
        mod = _modulation(c, w_ada[l], b_ada[l]).reshape(bsz, N_MOD_ROWS, d)
        x = _ffn(x, mod, ffn1_w_gu[l], ffn1_w_down[l], ln1_g[l], ln1_b[l], 0)
        qt, k, vt, h, ga, gb = _inproj(x, mod, w_in[l])
        attn = _attention(qt, k, vt)
        x = _mix(x, mod, attn, h, ga, gb, conv_w[l], conv_b[l], conv_ln_g[l], conv_ln_b[l],
                 w_sb_out[l], w_conv_out[l], w_out[l], ln2_g[l], ln2_b[l])
        x = _ffn(x, mod, ffn2_w_gu[l], ffn2_w_down[l], ln3_g[l], ln3_b[l], 2)
    return x
```

```python
import functools
import math

import jax
import jax.numpy as jnp
from jax import lax
from jax.experimental import pallas as pl
from jax.experimental.pallas import tpu as pltpu

F32 = jnp.float32
BF16 = jnp.bfloat16

D_MODEL = 1024
SB_HEADS = 16
SB_HEAD_DIM = 64
D_FF = 2816
CONV_WIDTH = 31
N_MOD_ROWS = 9
MACARON_WEIGHT = 0.5
DEEPNORM_ALPHA = 2.0 ** 0.25
LN_EPS = 1e-5

ATT_BLK = 256
ATT_TOTAL_ROWS = 16
HEAD_PAIR = 2 * SB_HEAD_DIM
ATT_DIAG_PER_ITER = 2
ATT_PAIRS_PER_ITER = 4
CONV_HALO = 32
LANES = 128

FFN_TM = 512
FFN_FW = 256
INPROJ_TM = 512
MIX_TM = 512
VMEM_LIMIT = 56 * 1024 * 1024

_NT = (((1,), (1,)), ((), ()))


def _layer_norm(r, g, b):
    mu = jnp.mean(r, axis=-1, keepdims=True)
    d = r - mu
    var = jnp.mean(d * d, axis=-1, keepdims=True)
    return d * lax.rsqrt(var + LN_EPS) * g + b


def _const_spec(shape):
    nd = len(shape)
    return pl.BlockSpec(shape, lambda *_: (0,) * nd, pipeline_mode=pl.Buffered(1))


def _mod_kernel(c_ref, w_ref, b_ref, o_ref):
    c = c_ref[...]
    s = c * jax.nn.sigmoid(c)
    o_ref[...] = jnp.dot(s, w_ref[...], preferred_element_type=F32,
                         precision=lax.Precision.HIGHEST) + b_ref[...]


def _modulation(c, w_ada, b_ada):
    bsz, d = c.shape
    n = w_ada.shape[1]
    tn = 1024
    return pl.pallas_call(
        _mod_kernel,
        grid=(n // tn,),
        in_specs=[pl.BlockSpec((bsz, d), lambda j: (0, 0)),
                  pl.BlockSpec((d, tn), lambda j: (0, j)),
                  pl.BlockSpec((1, tn), lambda j: (0, j))],
        out_specs=pl.BlockSpec((bsz, tn), lambda j: (0, j)),
        out_shape=jax.ShapeDtypeStruct((bsz, n), F32),
        compiler_params=pltpu.CompilerParams(dimension_semantics=("arbitrary",)),
        name="mod",
    )(c, w_ada, b_ada.reshape(1, n))


def _ffn_kernel(x_ref, mod_ref, wgu_ref, wd_ref, lng_ref, lnb_ref, o_ref, h_ref, *, sub):
    x = x_ref[...]
    sh = mod_ref[3 * sub:3 * sub + 1, :]
    sc = mod_ref[3 * sub + 1:3 * sub + 2, :]
    gate = mod_ref[3 * sub + 2:3 * sub + 3, :]
    u = (x * (1.0 + sc) + sh).astype(BF16)
    for c in range(D_FF // FFN_FW):
        lo = c * FFN_FW
        a = jnp.dot(u, wgu_ref[:, lo:lo + FFN_FW], preferred_element_type=F32)
        g = jnp.dot(u, wgu_ref[:, D_FF + lo:D_FF + lo + FFN_FW], preferred_element_type=F32)
        h_ref[:, lo:lo + FFN_FW] = (a * jax.nn.sigmoid(a) * g).astype(BF16)
    y = jnp.dot(h_ref[...], wd_ref[...], preferred_element_type=F32)
    r = DEEPNORM_ALPHA * x + gate * (MACARON_WEIGHT * y)
    o_ref[...] = _layer_norm(r, lng_ref[...], lnb_ref[...])


def _ffn(x, mod, w_gu, w_down, ln_g, ln_b, sub):
    bsz, seq, d = x.shape
    tm = FFN_TM
    return pl.pallas_call(
        functools.partial(_ffn_kernel, sub=sub),
        grid=(bsz, seq // tm),
        in_specs=[pl.BlockSpec((None, tm, d), lambda b, t: (b, t, 0)),
                  pl.BlockSpec((None, N_MOD_ROWS, d), lambda b, t: (b, 0, 0)),
                  _const_spec((d, 2 * D_FF)),
                  _const_spec((D_FF, d)),
                  _const_spec((1, d)),
                  _const_spec((1, d))],
        out_specs=pl.BlockSpec((None, tm, d), lambda b, t: (b, t, 0)),
        out_shape=jax.ShapeDtypeStruct((bsz, seq, d), F32),
        scratch_shapes=[pltpu.VMEM((tm, D_FF), BF16)],
        compiler_params=pltpu.CompilerParams(
            dimension_semantics=("arbitrary", "arbitrary"), vmem_limit_bytes=VMEM_LIMIT),
        name=f"ffn{sub}",
    )(x, mod, w_gu.astype(BF16), w_down.astype(BF16), ln_g.reshape(1, d), ln_b.reshape(1, d))


def _inproj_kernel(x_ref, mod_ref, wqt_ref, wk_ref, wvt_ref, wr_ref,
                   qt_ref, k_ref, vt_ref, h_ref, ga_ref, gb_ref):
    x = x_ref[...]
    sh = mod_ref[3:4, :]
    sc = mod_ref[4:5, :]
    u = (x * (1.0 + sc) + sh).astype(BF16)
    k_ref[...] = jnp.dot(u, wk_ref[...], preferred_element_type=F32).astype(BF16)
    qt = lax.dot_general(wqt_ref[...], u, _NT, preferred_element_type=F32)
    vt = lax.dot_general(wvt_ref[...], u, _NT, preferred_element_type=F32)
    for t in range(INPROJ_TM // ATT_BLK):
        qt_ref[t] = qt[:, t * ATT_BLK:(t + 1) * ATT_BLK].astype(BF16)
        vt_ref[t] = vt[:, t * ATT_BLK:(t + 1) * ATT_BLK].astype(BF16)
    d = D_MODEL
    a = jnp.dot(u, wr_ref[:, 0:d], preferred_element_type=F32)
    b = jnp.dot(u, wr_ref[:, d:2 * d], preferred_element_type=F32)
    h_ref[...] = a * jax.nn.sigmoid(b)
    ga_ref[...] = jax.nn.sigmoid(
        jnp.dot(u, wr_ref[:, 2 * d:3 * d], preferred_element_type=F32)).astype(BF16)
    gb_ref[...] = jax.nn.sigmoid(
        jnp.dot(u, wr_ref[:, 3 * d:4 * d], preferred_element_type=F32)).astype(BF16)


def _inproj(x, mod, w_in):
    bsz, seq, d = x.shape
    tm = INPROJ_TM
    nb = tm // ATT_BLK
    sbw = SB_HEADS * SB_HEAD_DIM
    scale = 1.0 / math.sqrt(SB_HEAD_DIM)
    wqt = (w_in[:, 0:sbw] * scale).T.astype(BF16)
    wk = w_in[:, sbw:2 * sbw].astype(BF16)
    wvt = w_in[:, 2 * sbw:3 * sbw].T.astype(BF16)
    wr = w_in[:, 3 * sbw:].astype(BF16)
    tok = pl.BlockSpec((None, tm, d), lambda b, t: (b, t, 0))
    blk_t = pl.BlockSpec((None, nb, sbw, ATT_BLK), lambda b, t: (b, t, 0, 0))
    return pl.pallas_call(
        _inproj_kernel,
        grid=(bsz, seq // tm),
        in_specs=[tok,
                  pl.BlockSpec((None, N_MOD_ROWS, d), lambda b, t: (b, 0, 0)),
                  _const_spec((sbw, d)), _const_spec((d, sbw)), _const_spec((sbw, d)),
                  _const_spec((d, 4 * d))],
        out_specs=[blk_t, tok, blk_t, tok, tok, tok],
        out_shape=[jax.ShapeDtypeStruct((bsz, seq // ATT_BLK, sbw, ATT_BLK), BF16),
                   jax.ShapeDtypeStruct((bsz, seq, sbw), BF16),
                   jax.ShapeDtypeStruct((bsz, seq // ATT_BLK, sbw, ATT_BLK), BF16),
                   jax.ShapeDtypeStruct((bsz, seq, d), F32),
                   jax.ShapeDtypeStruct((bsz, seq, d), BF16),
                   jax.ShapeDtypeStruct((bsz, seq, d), BF16)],
        compiler_params=pltpu.CompilerParams(
            dimension_semantics=("arbitrary", "arbitrary"), vmem_limit_bytes=VMEM_LIMIT),
        name="inproj",
    )(x, mod, wqt, wk, wvt, wr)


def _attn_kernel(qi_ref, kj_ref, qt_ref, k_ref, vt_ref, ut_ref, o_ref, qm_ref, acc_ref, later_ref):
    blk = ATT_BLK
    n_blk = qt_ref.shape[0]
    n_off = qi_ref.shape[0]
    dim_row = lax.broadcasted_iota(jnp.int32, (HEAD_PAIR, blk), 0)
    key_pos = lax.broadcasted_iota(jnp.int32, (blk, blk), 0)
    qry_pos = lax.broadcasted_iota(jnp.int32, (blk, blk), 1)
    causal = key_pos < qry_pos
    ut = ut_ref[...]

    def head_rows(head):
        return slice(head * SB_HEAD_DIM, (head + 1) * SB_HEAD_DIM)

    def blocks(work, masked):
        zts = [jnp.dot(k_ref[pl.ds(pl.multiple_of(j * blk, blk), blk), :], qh,
                       preferred_element_type=F32) for qh, _, j in work]
        log_betas, his, los = [], [], []
        for zt in zts:
            neg = jnp.minimum(zt, 0.0)
            npos = neg - zt
            sp = jnp.log(1.0 + jnp.exp(neg + npos))
            log_betas.append(neg - sp)
            log_keep = npos - sp
            if masked:
                log_keep = jnp.where(causal, log_keep, 0.0)
            hi = log_keep.astype(BF16)
            his.append(hi)
            los.append((log_keep - hi.astype(F32)).astype(BF16))
        css = [jnp.dot(ut, hi, preferred_element_type=F32) + jnp.dot(ut, lo, preferred_element_type=F32)
               for hi, lo in zip(his, los)]
        ws = []
        for log_beta, cs in zip(log_betas, css):
            w = jnp.exp(log_beta + cs[0:blk])
            if masked:
                w = jnp.where(causal, w, 0.0)
            ws.append(w.astype(BF16))
        pvs = [jnp.dot(vt_ref[j, head_rows(head), :], w, preferred_element_type=F32)
               for (_, head, j), w in zip(work, ws)]
        return [(pv, cs[blk:blk + 8]) for pv, cs in zip(pvs, css)]

    def diagonal_step(it, carry_unused):
        work = []
        for u in range(ATT_DIAG_PER_ITER):
            i = it + u * (n_blk // ATT_DIAG_PER_ITER)
            qt = qt_ref[i]
            for head in range(2):
                own = (dim_row >= SB_HEAD_DIM) if head else (dim_row < SB_HEAD_DIM)
                work.append((jnp.where(own, qt, jnp.zeros_like(qt)), head, i))
        for (qh, head, i), (pv, total) in zip(work, blocks(work, True)):
            qm_ref[i, head] = qh
            acc_ref[i, head_rows(head), :] = pv
            later_ref[i, head] = total
        return carry_unused

    def off_diagonal_step(it, carry_unused):
        work, dest = [], []
        for u in range(ATT_PAIRS_PER_ITER):
            n = it * ATT_PAIRS_PER_ITER + u
            i = qi_ref[n]
            j = kj_ref[n]
            for head in range(2):
                work.append((qm_ref[i, head], head, j))
                dest.append(i)
        for (_, head, _), i, (pv, total) in zip(work, dest, blocks(work, False)):
            later = later_ref[i, head]
            scale = jnp.concatenate([jnp.exp(later)] * (SB_HEAD_DIM // 8), axis=0)
            acc_ref[i, head_rows(head), :] = acc_ref[i, head_rows(head), :] + scale * pv
            later_ref[i, head] = later + total
        return carry_unused

    def output_step(i, carry_unused):
        o_ref[pl.ds(pl.multiple_of(i * blk, blk), blk), :] = acc_ref[i].T.astype(BF16)
        return carry_unused

    lax.fori_loop(0, n_blk // ATT_DIAG_PER_ITER, diagonal_step, 0)
    lax.fori_loop(0, n_off // ATT_PAIRS_PER_ITER, off_diagonal_step, 0)
    lax.fori_loop(0, n_blk, output_step, 0)


def _attention(qt, k, vt):
    bsz, n_blk, sbw, blk = qt.shape
    seq = k.shape[1]
    ut = jnp.concatenate(
        [jnp.triu(jnp.ones((blk, blk), F32), k=1), jnp.ones((ATT_TOTAL_ROWS, blk), F32)],
        axis=0).astype(BF16)
    pairs = [(i, j) for i in range(1, n_blk) for j in range(i - 1, -1, -1)]
    assert n_blk % ATT_DIAG_PER_ITER == 0 and len(pairs) % ATT_PAIRS_PER_ITER == 0
    qi = jnp.asarray([p[0] for p in pairs], jnp.int32)
    kj = jnp.asarray([p[1] for p in pairs], jnp.int32)
    n_pairs = sbw // HEAD_PAIR
    blk_t = pl.BlockSpec((None, n_blk, HEAD_PAIR, blk), lambda b, p, *_: (b, 0, p, 0))
    tok = pl.BlockSpec((None, seq, HEAD_PAIR), lambda b, p, *_: (b, 0, p))
    return pl.pallas_call(
        _attn_kernel,
        grid_spec=pltpu.PrefetchScalarGridSpec(
            num_scalar_prefetch=2,
            grid=(bsz, n_pairs),
            in_specs=[blk_t, tok, blk_t, _const_spec((blk + ATT_TOTAL_ROWS, blk))],
            out_specs=tok,
            scratch_shapes=[pltpu.VMEM((n_blk, 2, HEAD_PAIR, blk), BF16),
                            pltpu.VMEM((n_blk, HEAD_PAIR, blk), F32),
                            pltpu.VMEM((n_blk, 2, 8, blk), F32)]),
        out_shape=jax.ShapeDtypeStruct((bsz, seq, sbw), BF16),
        compiler_params=pltpu.CompilerParams(
            dimension_semantics=("arbitrary", "arbitrary"), vmem_limit_bytes=VMEM_LIMIT),
        name="attn",
    )(qi, kj, qt, k, vt, ut)


def _mix_kernel(x_ref, mod_ref, attn_ref, h_ref, halo_ref, ga_ref, gb_ref,
                cw_ref, cb_ref, clg_ref, clb_ref, wsb_ref, wco_ref, wo_ref, lng_ref, lnb_ref,
                o_ref, hext_ref, conv_ref):
    tm = x_ref.shape[0]
    first = pl.program_id(1) == 0
    base = CONV_HALO - (CONV_WIDTH - 1)
    for s in range(D_MODEL // LANES):
        lanes = slice(s * LANES, (s + 1) * LANES)
        halo = halo_ref[:, lanes]
        hext_ref[s, pl.ds(0, CONV_HALO, stride=2), :] = jnp.where(first, jnp.zeros_like(halo), halo)
        hext_ref[s, pl.ds(2 * CONV_HALO, tm, stride=2), :] = h_ref[:, lanes]
        conv = hext_ref[s, pl.ds(2 * base, tm, stride=2), :] * cw_ref[0:1, lanes] + cb_ref[:, lanes]
        for j in range(1, CONV_WIDTH):
            conv = conv + hext_ref[s, pl.ds(2 * (base + j), tm, stride=2), :] * cw_ref[j:j + 1, lanes]
        conv_ref[:, lanes] = conv
    cn = _layer_norm(conv_ref[...], clg_ref[...], clb_ref[...])
    act = (cn * jax.nn.sigmoid(cn)).astype(BF16)
    y_conv = jnp.dot(act, wco_ref[...], preferred_element_type=F32)
    y_sb = jnp.dot(attn_ref[...], wsb_ref[...], preferred_element_type=F32)
    merged = ga_ref[...].astype(F32) * y_sb + gb_ref[...].astype(F32) * y_conv
    o = jnp.dot(merged.astype(BF16), wo_ref[...], preferred_element_type=F32)
    x = x_ref[...]
    gate = mod_ref[5:6, :]
    r = DEEPNORM_ALPHA * x + gate * o
    o_ref[...] = _layer_norm(r, lng_ref[...], lnb_ref[...])


def _mix(x, mod, attn, h, ga, gb, conv_w, conv_b, conv_ln_g, conv_ln_b,
         w_sb_out, w_conv_out, w_out, ln_g, ln_b):
    bsz, seq, d = x.shape
    tm = MIX_TM
    halo_per_tile = tm // CONV_HALO
    tok = pl.BlockSpec((None, tm, d), lambda b, t: (b, t, 0))
    halo = pl.BlockSpec((None, CONV_HALO, d),
                        lambda b, t: (b, jnp.maximum(t * halo_per_tile - 1, 0), 0))
    vec = _const_spec((1, d))
    return pl.pallas_call(
        _mix_kernel,
        grid=(bsz, seq // tm),
        in_specs=[tok,
                  pl.BlockSpec((None, N_MOD_ROWS, d), lambda b, t: (b, 0, 0)),
                  tok, tok, halo, tok, tok,
                  _const_spec((CONV_WIDTH, d)), vec, vec, vec,
                  _const_spec((d, d)), _const_spec((d, d)), _const_spec((d, d)), vec, vec],
        out_specs=tok,
        out_shape=jax.ShapeDtypeStruct((bsz, seq, d), F32),
        scratch_shapes=[pltpu.VMEM((d // LANES, 2 * (tm + CONV_HALO), LANES), F32),
                        pltpu.VMEM((tm, d), F32)],
        compiler_params=pltpu.CompilerParams(
            dimension_semantics=("arbitrary", "arbitrary"), vmem_limit_bytes=VMEM_LIMIT),
        name="mix",
    )(x, mod, attn, h, h, ga, gb, conv_w, conv_b.reshape(1, d), conv_ln_g.reshape(1, d),
      conv_ln_b.reshape(1, d), w_sb_out.astype(BF16), w_conv_out.astype(BF16), w_out.astype(BF16),
      ln_g.reshape(1, d), ln_b.reshape(1, d))


def kernel(x, c, w_ada, b_ada, ffn1_w_gu, ffn1_w_down, ln1_g, ln1_b, w_in, w_sb_out, conv_w, conv_b,
           conv_ln_g, conv_ln_b, w_conv_out, w_out, ln2_g, ln2_b, ffn2_w_gu, ffn2_w_down, ln3_g, ln3_b):
    bsz, seq, d = x.shape
    depth = w_ada.shape[0]
    for l in range(depth):
        mod = _modulation(c, w_ada[l], b_ada[l]).reshape(bsz, N_MOD_ROWS, d)
        x = _ffn(x, mod, ffn1_w_gu[l], ffn1_w_down[l], ln1_g[l], ln1_b[l], 0)
        qt, k, vt, h, ga, gb = _inproj(x, mod, w_in[l])
        attn = _attention(qt, k, vt)
        x = _mix(x, mod, attn, h, ga, gb, conv_w[l], conv_b[l], conv_ln_g[l], conv_ln_b[l],
                 w_sb_out[l], w_conv_out[l], w_out[l], ln2_g[l], ln2_b[l])
        x = _ffn(x, mod, ffn2_w_gu[l], ffn2_w_down[l], ln3_g[l], ln3_b[l], 2)
    return x
```

```python
import functools
import math

import jax
import jax.numpy as jnp
from jax import lax
from jax.experimental import pallas as pl
from jax.experimental.pallas import tpu as pltpu

F32 = jnp.float32
BF16 = jnp.bfloat16

D_MODEL = 1024
SB_HEADS = 16
SB_HEAD_DIM = 64
D_FF = 2816
CONV_WIDTH = 31
N_MOD_ROWS = 9
MACARON_WEIGHT = 0.5
DEEPNORM_ALPHA = 2.0 ** 0.25
LN_EPS = 1e-5
LOG2_E = 1.0 / math.log(2.0)

ATT_BLK = 256
ATT_TOTAL_ROWS = 16
HEAD_PAIR = 2 * SB_HEAD_DIM
ATT_DIAG_PER_ITER = 4
ATT_PAIRS_PER_ITER = 4
CONV_HALO = 32
LANES = 128
CONV_ROWS = 64

FFN_TM = 512
FFN_FW = 256
INPROJ_TM = 512
INPROJ_GLU_CHUNK = 256
MIX_TM = 512
VMEM_LIMIT = 56 * 1024 * 1024

_NT = (((1,), (1,)), ((), ()))


def _layer_norm(r, g, b):
    mu = jnp.mean(r, axis=-1, keepdims=True)
    d = r - mu
    var = jnp.mean(d * d, axis=-1, keepdims=True)
    return d * lax.rsqrt(var + LN_EPS) * g + b


def _const_spec(shape):
    nd = len(shape)
    return pl.BlockSpec(shape, lambda *_: (0,) * nd, pipeline_mode=pl.Buffered(1))


def _mod_kernel(c_ref, w_ref, b_ref, o_ref):
    c = c_ref[...]
    s = c * jax.nn.sigmoid(c)
    o_ref[...] = jnp.dot(s, w_ref[...], preferred_element_type=F32,
                         precision=lax.Precision.HIGHEST) + b_ref[...]


def _modulation(c, w_ada, b_ada):
    bsz, d = c.shape
    n = w_ada.shape[1]
    tn = 1024
    return pl.pallas_call(
        _mod_kernel,
        grid=(n // tn,),
        in_specs=[pl.BlockSpec((bsz, d), lambda j: (0, 0)),
                  pl.BlockSpec((d, tn), lambda j: (0, j)),
                  pl.BlockSpec((1, tn), lambda j: (0, j))],
        out_specs=pl.BlockSpec((bsz, tn), lambda j: (0, j)),
        out_shape=jax.ShapeDtypeStruct((bsz, n), F32),
        compiler_params=pltpu.CompilerParams(dimension_semantics=("arbitrary",)),
        name="mod",
    )(c, w_ada, b_ada.reshape(1, n))


def _ffn_kernel(x_ref, mod_ref, wgu_ref, wd_ref, lng_ref, lnb_ref, o_ref, h_ref, *, sub):
    x = x_ref[...]
    sh = mod_ref[3 * sub:3 * sub + 1, :]
    sc = mod_ref[3 * sub + 1:3 * sub + 2, :]
    gate = mod_ref[3 * sub + 2:3 * sub + 3, :]
    u = (x * (1.0 + sc) + sh).astype(BF16)
    for c in range(D_FF // FFN_FW):
        lo = c * FFN_FW
        a = jnp.dot(u, wgu_ref[:, lo:lo + FFN_FW], preferred_element_type=F32)
        g = jnp.dot(u, wgu_ref[:, D_FF + lo:D_FF + lo + FFN_FW], preferred_element_type=F32)
        h_ref[:, lo:lo + FFN_FW] = (a * jax.nn.sigmoid(a) * g).astype(BF16)
    y = jnp.dot(h_ref[...], wd_ref[...], preferred_element_type=F32)
    r = DEEPNORM_ALPHA * x + gate * (MACARON_WEIGHT * y)
    o_ref[...] = _layer_norm(r, lng_ref[...], lnb_ref[...])


def _ffn(x, mod, w_gu, w_down, ln_g, ln_b, sub):
    bsz, seq, d = x.shape
    tm = FFN_TM
    return pl.pallas_call(
        functools.partial(_ffn_kernel, sub=sub),
        grid=(bsz, seq // tm),
        in_specs=[pl.BlockSpec((None, tm, d), lambda b, t: (b, t, 0)),
                  pl.BlockSpec((None, N_MOD_ROWS, d), lambda b, t: (b, 0, 0)),
                  _const_spec((d, 2 * D_FF)),
                  _const_spec((D_FF, d)),
                  _const_spec((1, d)),
                  _const_spec((1, d))],
        out_specs=pl.BlockSpec((None, tm, d), lambda b, t: (b, t, 0)),
        out_shape=jax.ShapeDtypeStruct((bsz, seq, d), F32),
        scratch_shapes=[pltpu.VMEM((tm, D_FF), BF16)],
        compiler_params=pltpu.CompilerParams(
            dimension_semantics=("arbitrary", "arbitrary"), vmem_limit_bytes=VMEM_LIMIT),
        name=f"ffn{sub}",
    )(x, mod, w_gu.astype(BF16), w_down.astype(BF16), ln_g.reshape(1, d), ln_b.reshape(1, d))


def _inproj_kernel(x_ref, mod_ref, wqt_ref, wk_ref, wvt_ref, wr_ref, cw_ref, cb_ref, clg_ref, clb_ref,
                   qt_ref, k_ref, vt_ref, act_ref, ga_ref, gb_ref, hext_ref, conv_ref):
    tm = x_ref.shape[0]
    n_slabs = D_MODEL // LANES
    first = pl.program_id(1) == 0

    @pl.when(first)
    def _zero_left_context():
        for s in range(n_slabs):
            hext_ref[s, pl.ds(0, CONV_HALO, stride=2), :] = jnp.zeros((CONV_HALO, LANES), F32)

    @pl.when(jnp.logical_not(first))
    def _carry_left_context():
        for s in range(n_slabs):
            hext_ref[s, pl.ds(0, CONV_HALO, stride=2), :] = hext_ref[s, pl.ds(2 * tm, CONV_HALO, stride=2), :]

    x = x_ref[...]
    sh = mod_ref[3:4, :]
    sc = mod_ref[4:5, :]
    u = (x * (1.0 + sc) + sh).astype(BF16)
    d = D_MODEL
    chunk = INPROJ_GLU_CHUNK
    n_chunks = d // chunk
    slabs_per_chunk = chunk // LANES
    n_tok = tm // ATT_BLK

    def glu_chunk(c):
        a = jnp.dot(u, wr_ref[:, c * chunk:(c + 1) * chunk], preferred_element_type=F32)
        b = jnp.dot(u, wr_ref[:, d + c * chunk:d + (c + 1) * chunk], preferred_element_type=F32)
        h = a * jax.nn.sigmoid(b)
        for s in range(slabs_per_chunk):
            hext_ref[c * slabs_per_chunk + s, pl.ds(2 * CONV_HALO, tm, stride=2), :] = (
                h[:, s * LANES:(s + 1) * LANES])

    def conv_slab(slab):
        base = CONV_HALO - (CONV_WIDTH - 1)
        lanes = slice(slab * LANES, (slab + 1) * LANES)
        taps = [cw_ref[j:j + 1, lanes] for j in range(CONV_WIDTH)]
        bias = cb_ref[:, lanes]
        for r0 in range(0, tm, CONV_ROWS):
            conv = hext_ref[slab, pl.ds(2 * (base + r0), CONV_ROWS, stride=2), :] * taps[0] + bias
            for j in range(1, CONV_WIDTH):
                conv = conv + hext_ref[slab, pl.ds(2 * (base + r0 + j), CONV_ROWS, stride=2), :] * taps[j]
            conv_ref[r0:r0 + CONV_ROWS, lanes] = conv

    def column_chunk(w_ref, w_col, out_ref, gate, c):
        y = jnp.dot(u, w_ref[:, w_col + c * chunk:w_col + (c + 1) * chunk], preferred_element_type=F32)
        out_ref[:, c * chunk:(c + 1) * chunk] = (jax.nn.sigmoid(y) if gate else y).astype(BF16)

    def token_chunk(wt_ref, out_ref, t):
        rows = slice(t * ATT_BLK, (t + 1) * ATT_BLK)
        out_ref[t] = lax.dot_general(wt_ref[...], u[rows, :], _NT,
                                     preferred_element_type=F32).astype(BF16)

    wide = ([functools.partial(token_chunk, wqt_ref, qt_ref, t) for t in range(n_tok)]
            + [functools.partial(token_chunk, wvt_ref, vt_ref, t) for t in range(n_tok)])
    narrow = ([functools.partial(column_chunk, wk_ref, 0, k_ref, False, c) for c in range(n_chunks)]
              + [functools.partial(column_chunk, wr_ref, 2 * d, ga_ref, True, c) for c in range(n_chunks)]
              + [functools.partial(column_chunk, wr_ref, 3 * d, gb_ref, True, c) for c in range(n_chunks)])
    n_slabs_total = n_chunks * slabs_per_chunk
    assert len(wide) * 2 == n_slabs_total and len(narrow) == len(wide) + n_slabs_total
    narrow_it = iter(narrow)
    projections = [[wide[slab // 2], next(narrow_it)] if slab % 2 == 0
                   else [next(narrow_it), next(narrow_it)] for slab in range(n_slabs_total)]

    glu_chunk(0)
    glu_chunk(1)
    for slab in range(n_slabs_total):
        c, s = divmod(slab, slabs_per_chunk)
        if s == 0 and c + 2 < n_chunks:
            glu_chunk(c + 2)
        conv_slab(slab)
        if slab == n_slabs_total - 1:
            cn = _layer_norm(conv_ref[...], clg_ref[...], clb_ref[...])
            act_ref[...] = (cn * jax.nn.sigmoid(cn)).astype(BF16)
        for proj in projections[slab]:
            proj()


def _inproj(x, mod, w_in, conv_w, conv_b, conv_ln_g, conv_ln_b):
    bsz, seq, d = x.shape
    tm = INPROJ_TM
    nb = tm // ATT_BLK
    sbw = SB_HEADS * SB_HEAD_DIM
    scale = LOG2_E / math.sqrt(SB_HEAD_DIM)
    wqt = (w_in[:, 0:sbw] * scale).T.astype(BF16)
    wk = w_in[:, sbw:2 * sbw].astype(BF16)
    wvt = w_in[:, 2 * sbw:3 * sbw].T.astype(BF16)
    wr = w_in[:, 3 * sbw:].astype(BF16)
    tok = pl.BlockSpec((None, tm, d), lambda b, t: (b, t, 0))
    blk_t = pl.BlockSpec((None, nb, sbw, ATT_BLK), lambda b, t: (b, t, 0, 0))
    vec = _const_spec((1, d))
    return pl.pallas_call(
        _inproj_kernel,
        grid=(bsz, seq // tm),
        in_specs=[tok,
                  pl.BlockSpec((None, N_MOD_ROWS, d), lambda b, t: (b, 0, 0)),
                  _const_spec((sbw, d)), _const_spec((d, sbw)), _const_spec((sbw, d)),
                  _const_spec((d, 4 * d)),
                  _const_spec((CONV_WIDTH, d)), vec, vec, vec],
        out_specs=[blk_t, tok, blk_t, tok, tok, tok],
        out_shape=[jax.ShapeDtypeStruct((bsz, seq // ATT_BLK, sbw, ATT_BLK), BF16),
                   jax.ShapeDtypeStruct((bsz, seq, sbw), BF16),
                   jax.ShapeDtypeStruct((bsz, seq // ATT_BLK, sbw, ATT_BLK), BF16),
                   jax.ShapeDtypeStruct((bsz, seq, d), BF16),
                   jax.ShapeDtypeStruct((bsz, seq, d), BF16),
                   jax.ShapeDtypeStruct((bsz, seq, d), BF16)],
        scratch_shapes=[pltpu.VMEM((d // LANES, 2 * (CONV_HALO + tm), LANES), F32),
                        pltpu.VMEM((tm, d), F32)],
        compiler_params=pltpu.CompilerParams(
            dimension_semantics=("arbitrary", "arbitrary"), vmem_limit_bytes=VMEM_LIMIT),
        name="inproj",
    )(x, mod, wqt, wk, wvt, wr, conv_w, conv_b.reshape(1, d), conv_ln_g.reshape(1, d),
      conv_ln_b.reshape(1, d))


def _attn_kernel(qi_ref, kj_ref, qt_ref, k_ref, vt_ref, ut_ref, o_ref, qm_ref, acc_ref, later_ref):
    blk = ATT_BLK
    n_blk = qt_ref.shape[0]
    n_off = qi_ref.shape[0]
    dim_row = lax.broadcasted_iota(jnp.int32, (HEAD_PAIR, blk), 0)
    key_pos = lax.broadcasted_iota(jnp.int32, (blk, blk), 0)
    qry_pos = lax.broadcasted_iota(jnp.int32, (blk, blk), 1)
    causal = key_pos < qry_pos
    ut = ut_ref[...]

    def head_rows(head):
        return slice(head * SB_HEAD_DIM, (head + 1) * SB_HEAD_DIM)

    def blocks(work, masked):
        kjs = [k_ref[pl.ds(pl.multiple_of(j * blk, blk), blk), :] for _, _, j in work]
        zts = [jnp.dot(kj, qh, preferred_element_type=F32)
               for kj, (qh, _, _) in zip(kjs, work)]
        log_keeps = []
        for zt in zts:
            neg = jnp.minimum(zt, 0.0)
            npos = neg - zt
            sp = jnp.log(1.0 + jnp.exp2(neg + npos)) * LOG2_E
            log_keep = npos - sp
            if masked:
                log_keep = jnp.where(causal, log_keep, 0.0)
            log_keeps.append(log_keep.astype(BF16))
        pad = jnp.zeros((ATT_TOTAL_ROWS, HEAD_PAIR), BF16)
        exps = [jnp.dot(ut, lk, preferred_element_type=F32)
                + jnp.dot(jnp.concatenate([kj, pad], axis=0), qh, preferred_element_type=F32)
                for lk, kj, (qh, _, _) in zip(log_keeps, kjs, work)]
        ws = []
        for ex in exps:
            w = jnp.exp2(ex[0:blk])
            if masked:
                w = jnp.where(causal, w, 0.0)
            ws.append(w.astype(BF16))
        pvs = [jnp.dot(vt_ref[j, head_rows(head), :], w, preferred_element_type=F32)
               for (_, head, j), w in zip(work, ws)]
        return [(pv, ex[blk:blk + 8]) for pv, ex in zip(pvs, exps)]

    def diagonal_step(it, carry_unused):
        work = []
        for u in range(ATT_DIAG_PER_ITER):
            i = it + u * (n_blk // ATT_DIAG_PER_ITER)
            qt = qt_ref[i]
            for head in range(2):
                own = (dim_row >= SB_HEAD_DIM) if head else (dim_row < SB_HEAD_DIM)
                work.append((jnp.where(own, qt, jnp.zeros_like(qt)), head, i))
        for (qh, head, i), (pv, total) in zip(work, blocks(work, True)):
            qm_ref[i, head] = qh
            acc_ref[i, head_rows(head), :] = pv
            later_ref[i, head] = total
        return carry_unused

    def off_diagonal_step(it, carry_unused):
        work, dest = [], []
        for u in range(ATT_PAIRS_PER_ITER):
            n = it * ATT_PAIRS_PER_ITER + u
            i = qi_ref[n]
            j = kj_ref[n]
            for head in range(2):
                work.append((qm_ref[i, head], head, j))
                dest.append(i)
        for (_, head, _), i, (pv, total) in zip(work, dest, blocks(work, False)):
            later = later_ref[i, head]
            scale = jnp.concatenate([jnp.exp2(later)] * (SB_HEAD_DIM // 8), axis=0)
            acc_ref[i, head_rows(head), :] = acc_ref[i, head_rows(head), :] + scale * pv
            later_ref[i, head] = later + total
        return carry_unused

    def output_step(i, carry_unused):
        o_ref[pl.ds(pl.multiple_of(i * blk, blk), blk), :] = acc_ref[i].T.astype(BF16)
        return carry_unused

    lax.fori_loop(0, n_blk // ATT_DIAG_PER_ITER, diagonal_step, 0)
    lax.fori_loop(0, n_off // ATT_PAIRS_PER_ITER, off_diagonal_step, 0)
    lax.fori_loop(0, n_blk, output_step, 0)


def _attention(qt, k, vt):
    bsz, n_blk, sbw, blk = qt.shape
    seq = k.shape[1]
    ut = jnp.concatenate(
        [jnp.triu(jnp.ones((blk, blk), F32)), jnp.ones((ATT_TOTAL_ROWS, blk), F32)],
        axis=0).astype(BF16)
    pairs = [(i, j) for i in range(1, n_blk) for j in range(i - 1, -1, -1)]
    assert n_blk % ATT_DIAG_PER_ITER == 0 and len(pairs) % ATT_PAIRS_PER_ITER == 0
    qi = jnp.asarray([p[0] for p in pairs], jnp.int32)
    kj = jnp.asarray([p[1] for p in pairs], jnp.int32)
    n_pairs = sbw // HEAD_PAIR
    blk_t = pl.BlockSpec((None, n_blk, HEAD_PAIR, blk), lambda b, p, *_: (b, 0, p, 0))
    tok = pl.BlockSpec((None, seq, HEAD_PAIR), lambda b, p, *_: (b, 0, p))
    return pl.pallas_call(
        _attn_kernel,
        grid_spec=pltpu.PrefetchScalarGridSpec(
            num_scalar_prefetch=2,
            grid=(bsz, n_pairs),
            in_specs=[blk_t, tok, blk_t, _const_spec((blk + ATT_TOTAL_ROWS, blk))],
            out_specs=tok,
            scratch_shapes=[pltpu.VMEM((n_blk, 2, HEAD_PAIR, blk), BF16),
                            pltpu.VMEM((n_blk, HEAD_PAIR, blk), F32),
                            pltpu.VMEM((n_blk, 2, 8, blk), F32)]),
        out_shape=jax.ShapeDtypeStruct((bsz, seq, sbw), BF16),
        compiler_params=pltpu.CompilerParams(
            dimension_semantics=("arbitrary", "arbitrary"), vmem_limit_bytes=VMEM_LIMIT),
        name="attn",
    )(qi, kj, qt, k, vt, ut)


def _mix_kernel(x_ref, mod_ref, attn_ref, act_ref, ga_ref, gb_ref,
                wsb_ref, wco_ref, wo_ref, lng_ref, lnb_ref, o_ref):
    y_conv = jnp.dot(act_ref[...], wco_ref[...], preferred_element_type=F32)
    y_sb = jnp.dot(attn_ref[...], wsb_ref[...], preferred_element_type=F32)
    merged = ga_ref[...].astype(F32) * y_sb + gb_ref[...].astype(F32) * y_conv
    o = jnp.dot(merged.astype(BF16), wo_ref[...], preferred_element_type=F32)
    x = x_ref[...]
    gate = mod_ref[5:6, :]
    r = DEEPNORM_ALPHA * x + gate * o
    o_ref[...] = _layer_norm(r, lng_ref[...], lnb_ref[...])


def _mix(x, mod, attn, act, ga, gb, w_sb_out, w_conv_out, w_out, ln_g, ln_b):
    bsz, seq, d = x.shape
    tm = MIX_TM
    tok = pl.BlockSpec((None, tm, d), lambda b, t: (b, t, 0))
    vec = _const_spec((1, d))
    return pl.pallas_call(
        _mix_kernel,
        grid=(bsz, seq // tm),
        in_specs=[tok,
                  pl.BlockSpec((None, N_MOD_ROWS, d), lambda b, t: (b, 0, 0)),
                  tok, tok, tok, tok,
                  _const_spec((d, d)), _const_spec((d, d)), _const_spec((d, d)), vec, vec],
        out_specs=tok,
        out_shape=jax.ShapeDtypeStruct((bsz, seq, d), F32),
        compiler_params=pltpu.CompilerParams(
            dimension_semantics=("arbitrary", "arbitrary"), vmem_limit_bytes=VMEM_LIMIT),
        name="mix",
    )(x, mod, attn, act, ga, gb, w_sb_out.astype(BF16), w_conv_out.astype(BF16), w_out.astype(BF16),
      ln_g.reshape(1, d), ln_b.reshape(1, d))


def kernel(x, c, w_ada, b_ada, ffn1_w_gu, ffn1_w_down, ln1_g, ln1_b, w_in, w_sb_out, conv_w, conv_b,
           conv_ln_g, conv_ln_b, w_conv_out, w_out, ln2_g, ln2_b, ffn2_w_gu, ffn2_w_down, ln3_g, ln3_b):
    bsz, seq, d = x.shape
    depth = w_ada.shape[0]
    for l in range(depth):
        mod = _modulation(c, w_ada[l], b_ada[l]).reshape(bsz, N_MOD_ROWS, d)
        x = _ffn(x, mod, ffn1_w_gu[l], ffn1_w_down[l], ln1_g[l], ln1_b[l], 0)
        qt, k, vt, act, ga, gb = _inproj(x, mod, w_in[l], conv_w[l], conv_b[l], conv_ln_g[l], conv_ln_b[l])
        attn = _attention(qt, k, vt)
        x = _mix(x, mod, attn, act, ga, gb, w_sb_out[l], w_conv_out[l], w_out[l], ln2_g[l], ln2_b[l])
        x = _ffn(x, mod, ffn2_w_gu[l], ffn2_w_down[l], ln3_g[l], ln3_b[l], 2)
    return x
```

```python
import functools
import math

import jax
import jax.numpy as jnp
from jax import lax
from jax.experimental import pallas as pl
from jax.experimental.pallas import tpu as pltpu

F32 = jnp.float32
BF16 = jnp.bfloat16

D_MODEL = 1024
SB_HEADS = 16
SB_HEAD_DIM = 64
D_FF = 2816
CONV_WIDTH = 31
N_MOD_ROWS = 9
MACARON_WEIGHT = 0.5
DEEPNORM_ALPHA = 2.0 ** 0.25
LN_EPS = 1e-5
LOG2_E = 1.0 / math.log(2.0)

ATT_BLK = 256
ATT_TOTAL_ROWS = 16
HEAD_PAIR = 2 * SB_HEAD_DIM
ATT_DIAG_PER_ITER = 4
ATT_PAIRS_PER_ITER = 4
CONV_HALO = 32
LANES = 128
CONV_ROWS = 64

FFN_TM = 512
FFN_FW = 256
INPROJ_TM = 512
INPROJ_GLU_CHUNK = 256
MIX_TM = 512
VMEM_LIMIT = 56 * 1024 * 1024

_NT = (((1,), (1,)), ((), ()))
_TN = (((0,), (0,)), ((), ()))


def _layer_norm(r, g, b):
    mu = jnp.mean(r, axis=-1, keepdims=True)
    d = r - mu
    var = jnp.mean(d * d, axis=-1, keepdims=True)
    return d * lax.rsqrt(var + LN_EPS) * g + b


def _const_spec(shape):
    nd = len(shape)
    return pl.BlockSpec(shape, lambda *_: (0,) * nd, pipeline_mode=pl.Buffered(1))


def _mod_kernel(c_ref, w_ref, b_ref, o_ref):
    c = c_ref[...]
    s = c * jax.nn.sigmoid(c)
    o_ref[...] = jnp.dot(s, w_ref[...], preferred_element_type=F32,
                         precision=lax.Precision.HIGHEST) + b_ref[...]


def _modulation(c, w_ada, b_ada):
    bsz, d = c.shape
    n = w_ada.shape[1]
    tn = 1024
    return pl.pallas_call(
        _mod_kernel,
        grid=(n // tn,),
        in_specs=[pl.BlockSpec((bsz, d), lambda j: (0, 0)),
                  pl.BlockSpec((d, tn), lambda j: (0, j)),
                  pl.BlockSpec((1, tn), lambda j: (0, j))],
        out_specs=pl.BlockSpec((bsz, tn), lambda j: (0, j)),
        out_shape=jax.ShapeDtypeStruct((bsz, n), F32),
        compiler_params=pltpu.CompilerParams(dimension_semantics=("arbitrary",)),
        name="mod",
    )(c, w_ada, b_ada.reshape(1, n))


def _ffn_kernel(x_ref, mod_ref, wgu_ref, wd_ref, lng_ref, lnb_ref, o_ref, h_ref, *, sub):
    x = x_ref[...]
    sh = mod_ref[3 * sub:3 * sub + 1, :]
    sc = mod_ref[3 * sub + 1:3 * sub + 2, :]
    gate = mod_ref[3 * sub + 2:3 * sub + 3, :]
    u = (x * (1.0 + sc) + sh).astype(BF16)
    for c in range(D_FF // FFN_FW):
        lo = c * FFN_FW
        a = jnp.dot(u, wgu_ref[:, lo:lo + FFN_FW], preferred_element_type=F32)
        g = jnp.dot(u, wgu_ref[:, D_FF + lo:D_FF + lo + FFN_FW], preferred_element_type=F32)
        h_ref[:, lo:lo + FFN_FW] = (a * jax.nn.sigmoid(a) * g).astype(BF16)
    y = jnp.dot(h_ref[...], wd_ref[...], preferred_element_type=F32)
    r = DEEPNORM_ALPHA * x + gate * (MACARON_WEIGHT * y)
    o_ref[...] = _layer_norm(r, lng_ref[...], lnb_ref[...])


def _ffn(x, mod, w_gu, w_down, ln_g, ln_b, sub):
    bsz, seq, d = x.shape
    tm = FFN_TM
    return pl.pallas_call(
        functools.partial(_ffn_kernel, sub=sub),
        grid=(bsz, seq // tm),
        in_specs=[pl.BlockSpec((None, tm, d), lambda b, t: (b, t, 0)),
                  pl.BlockSpec((None, N_MOD_ROWS, d), lambda b, t: (b, 0, 0)),
                  _const_spec((d, 2 * D_FF)),
                  _const_spec((D_FF, d)),
                  _const_spec((1, d)),
                  _const_spec((1, d))],
        out_specs=pl.BlockSpec((None, tm, d), lambda b, t: (b, t, 0)),
        out_shape=jax.ShapeDtypeStruct((bsz, seq, d), F32),
        scratch_shapes=[pltpu.VMEM((tm, D_FF), BF16)],
        compiler_params=pltpu.CompilerParams(
            dimension_semantics=("arbitrary", "arbitrary"), vmem_limit_bytes=VMEM_LIMIT),
        name=f"ffn{sub}",
    )(x, mod, w_gu.astype(BF16), w_down.astype(BF16), ln_g.reshape(1, d), ln_b.reshape(1, d))


def _inproj_kernel(x_ref, mod_ref, wqt_ref, wk_ref, wvt_ref, wr_ref, cw_ref, cb_ref, clg_ref, clb_ref,
                   qt_ref, k_ref, vt_ref, act_ref, ga_ref, gb_ref, hext_ref, conv_ref):
    tm = x_ref.shape[0]
    n_slabs = D_MODEL // LANES
    first = pl.program_id(1) == 0

    @pl.when(first)
    def _zero_left_context():
        for s in range(n_slabs):
            hext_ref[s, pl.ds(0, CONV_HALO, stride=2), :] = jnp.zeros((CONV_HALO, LANES), F32)

    @pl.when(jnp.logical_not(first))
    def _carry_left_context():
        for s in range(n_slabs):
            hext_ref[s, pl.ds(0, CONV_HALO, stride=2), :] = hext_ref[s, pl.ds(2 * tm, CONV_HALO, stride=2), :]

    x = x_ref[...]
    sh = mod_ref[3:4, :]
    sc = mod_ref[4:5, :]
    u = (x * (1.0 + sc) + sh).astype(BF16)
    d = D_MODEL
    chunk = INPROJ_GLU_CHUNK
    n_chunks = d // chunk
    slabs_per_chunk = chunk // LANES
    n_tok = tm // ATT_BLK

    def glu_chunk(c):
        a = jnp.dot(u, wr_ref[:, c * chunk:(c + 1) * chunk], preferred_element_type=F32)
        b = jnp.dot(u, wr_ref[:, d + c * chunk:d + (c + 1) * chunk], preferred_element_type=F32)
        h = a * jax.nn.sigmoid(b)
        for s in range(slabs_per_chunk):
            hext_ref[c * slabs_per_chunk + s, pl.ds(2 * CONV_HALO, tm, stride=2), :] = (
                h[:, s * LANES:(s + 1) * LANES])

    def conv_slab(slab):
        base = CONV_HALO - (CONV_WIDTH - 1)
        lanes = slice(slab * LANES, (slab + 1) * LANES)
        taps = [cw_ref[j:j + 1, lanes] for j in range(CONV_WIDTH)]
        bias = cb_ref[:, lanes]
        for r0 in range(0, tm, CONV_ROWS):
            conv = hext_ref[slab, pl.ds(2 * (base + r0), CONV_ROWS, stride=2), :] * taps[0] + bias
            for j in range(1, CONV_WIDTH):
                conv = conv + hext_ref[slab, pl.ds(2 * (base + r0 + j), CONV_ROWS, stride=2), :] * taps[j]
            conv_ref[r0:r0 + CONV_ROWS, lanes] = conv

    def column_chunk(w_ref, w_col, out_ref, gate, c):
        y = jnp.dot(u, w_ref[:, w_col + c * chunk:w_col + (c + 1) * chunk], preferred_element_type=F32)
        out_ref[:, c * chunk:(c + 1) * chunk] = (jax.nn.sigmoid(y) if gate else y).astype(BF16)

    def token_chunk(wt_ref, out_ref, t):
        rows = slice(t * ATT_BLK, (t + 1) * ATT_BLK)
        out_ref[t] = lax.dot_general(wt_ref[...], u[rows, :], _NT,
                                     preferred_element_type=F32).astype(BF16)

    wide = ([functools.partial(token_chunk, wqt_ref, qt_ref, t) for t in range(n_tok)]
            + [functools.partial(token_chunk, wvt_ref, vt_ref, t) for t in range(n_tok)])
    narrow = ([functools.partial(column_chunk, wk_ref, 0, k_ref, False, c) for c in range(n_chunks)]
              + [functools.partial(column_chunk, wr_ref, 2 * d, ga_ref, True, c) for c in range(n_chunks)]
              + [functools.partial(column_chunk, wr_ref, 3 * d, gb_ref, True, c) for c in range(n_chunks)])
    n_slabs_total = n_chunks * slabs_per_chunk
    assert len(wide) * 2 == n_slabs_total and len(narrow) == len(wide) + n_slabs_total
    narrow_it = iter(narrow)
    projections = [[wide[slab // 2], next(narrow_it)] if slab % 2 == 0
                   else [next(narrow_it), next(narrow_it)] for slab in range(n_slabs_total)]

    glu_chunk(0)
    glu_chunk(1)
    for slab in range(n_slabs_total):
        c, s = divmod(slab, slabs_per_chunk)
        if s == 0 and c + 2 < n_chunks:
            glu_chunk(c + 2)
        conv_slab(slab)
        if slab == n_slabs_total - 1:
            cn = _layer_norm(conv_ref[...], clg_ref[...], clb_ref[...])
            act_ref[...] = (cn * jax.nn.sigmoid(cn)).astype(BF16)
        for proj in projections[slab]:
            proj()


def _inproj(x, mod, w_in, conv_w, conv_b, conv_ln_g, conv_ln_b):
    bsz, seq, d = x.shape
    tm = INPROJ_TM
    nb = tm // ATT_BLK
    sbw = SB_HEADS * SB_HEAD_DIM
    scale = LOG2_E / math.sqrt(SB_HEAD_DIM)
    wqt = (w_in[:, 0:sbw] * scale).T.astype(BF16)
    wk = w_in[:, sbw:2 * sbw].astype(BF16)
    wvt = w_in[:, 2 * sbw:3 * sbw].T.astype(BF16)
    wr = w_in[:, 3 * sbw:].astype(BF16)
    tok = pl.BlockSpec((None, tm, d), lambda b, t: (b, t, 0))
    blk_t = pl.BlockSpec((None, nb, sbw, ATT_BLK), lambda b, t: (b, t, 0, 0))
    vec = _const_spec((1, d))
    return pl.pallas_call(
        _inproj_kernel,
        grid=(bsz, seq // tm),
        in_specs=[tok,
                  pl.BlockSpec((None, N_MOD_ROWS, d), lambda b, t: (b, 0, 0)),
                  _const_spec((sbw, d)), _const_spec((d, sbw)), _const_spec((sbw, d)),
                  _const_spec((d, 4 * d)),
                  _const_spec((CONV_WIDTH, d)), vec, vec, vec],
        out_specs=[blk_t, tok, blk_t, tok, tok, tok],
        out_shape=[jax.ShapeDtypeStruct((bsz, seq // ATT_BLK, sbw, ATT_BLK), BF16),
                   jax.ShapeDtypeStruct((bsz, seq, sbw), BF16),
                   jax.ShapeDtypeStruct((bsz, seq // ATT_BLK, sbw, ATT_BLK), BF16),
                   jax.ShapeDtypeStruct((bsz, seq, d), BF16),
                   jax.ShapeDtypeStruct((bsz, seq, d), BF16),
                   jax.ShapeDtypeStruct((bsz, seq, d), BF16)],
        scratch_shapes=[pltpu.VMEM((d // LANES, 2 * (CONV_HALO + tm), LANES), F32),
                        pltpu.VMEM((tm, d), F32)],
        compiler_params=pltpu.CompilerParams(
            dimension_semantics=("arbitrary", "arbitrary"), vmem_limit_bytes=VMEM_LIMIT),
        name="inproj",
    )(x, mod, wqt, wk, wvt, wr, conv_w, conv_b.reshape(1, d), conv_ln_g.reshape(1, d),
      conv_ln_b.reshape(1, d))


def _attn_kernel(qi_ref, kj_ref, qt_ref, k_ref, vt_ref, ut_ref, o_ref, qm_ref, acc_ref, later_ref):
    blk = ATT_BLK
    n_blk = qt_ref.shape[0]
    n_off = qi_ref.shape[0]
    dim_row = lax.broadcasted_iota(jnp.int32, (HEAD_PAIR, blk), 0)
    key_pos = lax.broadcasted_iota(jnp.int32, (blk, blk), 0)
    qry_pos = lax.broadcasted_iota(jnp.int32, (blk, blk), 1)
    causal = key_pos < qry_pos
    half = blk // 2
    ut_tri = ut_ref[0:half, 0:half]
    ut_tail = ut_ref[half:half + ATT_TOTAL_ROWS, :]

    def head_rows(head):
        return slice(head * SB_HEAD_DIM, (head + 1) * SB_HEAD_DIM)

    def blocks(work, masked):
        kjs = [k_ref[pl.ds(pl.multiple_of(j * blk, blk), blk), :] for _, _, j in work]
        zts = [jnp.dot(kj, qh, preferred_element_type=F32)
               for kj, (qh, _, _) in zip(kjs, work)]
        log_keeps = []
        for zt in zts:
            neg = jnp.minimum(zt, 0.0)
            npos = neg - zt
            sp = jnp.log(1.0 + jnp.exp2(neg + npos)) * LOG2_E
            log_keep = npos - sp
            if masked:
                log_keep = jnp.where(causal, log_keep, 0.0)
            log_keeps.append(log_keep.astype(BF16))
        halves = []
        for lk, kj, (qh, _, _) in zip(log_keeps, kjs, work):
            pair = []
            for rows in (slice(0, half), slice(half, blk)):
                lhs = jnp.concatenate(
                    [jnp.concatenate([ut_tri, kj[rows]], axis=1), ut_tail], axis=0)
                rhs = jnp.concatenate([lk[rows], qh], axis=0)
                pair.append(jnp.dot(lhs, rhs, preferred_element_type=F32))
            halves.append(pair)
        ws, totals = [], []
        for ex_lo, ex_hi in halves:
            total_hi = ex_hi[half:half + 8]
            behind = jnp.concatenate([total_hi] * (half // 8), axis=0)
            w = jnp.exp2(jnp.concatenate([ex_lo[0:half] + behind, ex_hi[0:half]], axis=0))
            if masked:
                w = jnp.where(causal, w, 0.0)
            ws.append(w.astype(BF16))
            totals.append(total_hi + ex_lo[half:half + 8])
        pvs = [jnp.dot(vt_ref[j, head_rows(head), :], w, preferred_element_type=F32)
               for (_, head, j), w in zip(work, ws)]
        return list(zip(pvs, totals))

    def diagonal_step(it, carry_unused):
        work = []
        for u in range(ATT_DIAG_PER_ITER):
            i = it + u * (n_blk // ATT_DIAG_PER_ITER)
            qt = qt_ref[i]
            for head in range(2):
                own = (dim_row >= SB_HEAD_DIM) if head else (dim_row < SB_HEAD_DIM)
                work.append((jnp.where(own, qt, jnp.zeros_like(qt)), head, i))
        for (qh, head, i), (pv, total) in zip(work, blocks(work, True)):
            qm_ref[i, head] = qh
            acc_ref[i, head_rows(head), :] = pv
            later_ref[i, head] = total
        return carry_unused

    def off_diagonal_step(it, carry_unused):
        work, dest = [], []
        for u in range(ATT_PAIRS_PER_ITER):
            n = it * ATT_PAIRS_PER_ITER + u
            i = qi_ref[n]
            j = kj_ref[n]
            for head in range(2):
                work.append((qm_ref[i, head], head, j))
                dest.append(i)
        for (_, head, _), i, (pv, total) in zip(work, dest, blocks(work, False)):
            later = later_ref[i, head]
            scale = jnp.concatenate([jnp.exp2(later)] * (SB_HEAD_DIM // 8), axis=0)
            acc_ref[i, head_rows(head), :] = acc_ref[i, head_rows(head), :] + scale * pv
            later_ref[i, head] = later + total
        return carry_unused

    lax.fori_loop(0, n_blk // ATT_DIAG_PER_ITER, diagonal_step, 0)
    lax.fori_loop(0, n_off // ATT_PAIRS_PER_ITER, off_diagonal_step, 0)
    o_ref[...] = acc_ref[...].astype(BF16)


def _attention(qt, k, vt):
    bsz, n_blk, sbw, blk = qt.shape
    seq = k.shape[1]
    half = blk // 2
    ut = jnp.concatenate(
        [jnp.triu(jnp.ones((half, half), F32)), jnp.ones((ATT_TOTAL_ROWS, half), F32)], axis=0)
    ut = jnp.concatenate([ut, jnp.zeros_like(ut)], axis=1).astype(BF16)
    pairs = [(i, j) for i in range(1, n_blk) for j in range(i - 1, -1, -1)]
    assert n_blk % ATT_DIAG_PER_ITER == 0 and len(pairs) % ATT_PAIRS_PER_ITER == 0
    qi = jnp.asarray([p[0] for p in pairs], jnp.int32)
    kj = jnp.asarray([p[1] for p in pairs], jnp.int32)
    n_pairs = sbw // HEAD_PAIR
    blk_t = pl.BlockSpec((None, n_blk, HEAD_PAIR, blk), lambda b, p, *_: (b, 0, p, 0))
    tok = pl.BlockSpec((None, seq, HEAD_PAIR), lambda b, p, *_: (b, 0, p))
    return pl.pallas_call(
        _attn_kernel,
        grid_spec=pltpu.PrefetchScalarGridSpec(
            num_scalar_prefetch=2,
            grid=(bsz, n_pairs),
            in_specs=[blk_t, tok, blk_t, _const_spec((half + ATT_TOTAL_ROWS, blk))],
            out_specs=blk_t,
            scratch_shapes=[pltpu.VMEM((n_blk, 2, HEAD_PAIR, blk), BF16),
                            pltpu.VMEM((n_blk, HEAD_PAIR, blk), F32),
                            pltpu.VMEM((n_blk, 2, 8, blk), F32)]),
        out_shape=jax.ShapeDtypeStruct((bsz, n_blk, sbw, blk), BF16),
        compiler_params=pltpu.CompilerParams(
            dimension_semantics=("arbitrary", "arbitrary"), vmem_limit_bytes=VMEM_LIMIT),
        name="attn",
    )(qi, kj, qt, k, vt, ut)


def _mix_kernel(x_ref, mod_ref, attn_ref, act_ref, ga_ref, gb_ref,
                wsb_ref, wco_ref, wo_ref, lng_ref, lnb_ref, o_ref):
    y_conv = jnp.dot(act_ref[...], wco_ref[...], preferred_element_type=F32)
    y_sb = jnp.concatenate(
        [lax.dot_general(attn_ref[t], wsb_ref[...], _TN, preferred_element_type=F32)
         for t in range(attn_ref.shape[0])], axis=0)
    merged = ga_ref[...].astype(F32) * y_sb + gb_ref[...].astype(F32) * y_conv
    o = jnp.dot(merged.astype(BF16), wo_ref[...], preferred_element_type=F32)
    x = x_ref[...]
    gate = mod_ref[5:6, :]
    r = DEEPNORM_ALPHA * x + gate * o
    o_ref[...] = _layer_norm(r, lng_ref[...], lnb_ref[...])


def _mix(x, mod, attn, act, ga, gb, w_sb_out, w_conv_out, w_out, ln_g, ln_b):
    bsz, seq, d = x.shape
    tm = MIX_TM
    tok = pl.BlockSpec((None, tm, d), lambda b, t: (b, t, 0))
    attn_t = pl.BlockSpec((None, tm // ATT_BLK, attn.shape[2], ATT_BLK), lambda b, t: (b, t, 0, 0))
    vec = _const_spec((1, d))
    return pl.pallas_call(
        _mix_kernel,
        grid=(bsz, seq // tm),
        in_specs=[tok,
                  pl.BlockSpec((None, N_MOD_ROWS, d), lambda b, t: (b, 0, 0)),
                  attn_t, tok, tok, tok,
                  _const_spec((d, d)), _const_spec((d, d)), _const_spec((d, d)), vec, vec],
        out_specs=tok,
        out_shape=jax.ShapeDtypeStruct((bsz, seq, d), F32),
        compiler_params=pltpu.CompilerParams(
            dimension_semantics=("arbitrary", "arbitrary"), vmem_limit_bytes=VMEM_LIMIT),
        name="mix",
    )(x, mod, attn, act, ga, gb, w_sb_out.astype(BF16), w_conv_out.astype(BF16), w_out.astype(BF16),
      ln_g.reshape(1, d), ln_b.reshape(1, d))


def kernel(x, c, w_ada, b_ada, ffn1_w_gu, ffn1_w_down, ln1_g, ln1_b, w_in, w_sb_out, conv_w, conv_b,
           conv_ln_g, conv_ln_b, w_conv_out, w_out, ln2_g, ln2_b, ffn2_w_gu, ffn2_w_down, ln3_g, ln3_b):
    bsz, seq, d = x.shape
    depth = w_ada.shape[0]
    for l in range(depth):
        mod = _modulation(c, w_ada[l], b_ada[l]).reshape(bsz, N_MOD_ROWS, d)
        x = _ffn(x, mod, ffn1_w_gu[l], ffn1_w_down[l], ln1_g[l], ln1_b[l], 0)
        qt, k, vt, act, ga, gb = _inproj(x, mod, w_in[l], conv_w[l], conv_b[l], conv_ln_g[l], conv_ln_b[l])
        attn = _attention(qt, k, vt)
        x = _mix(x, mod, attn, act, ga, gb, w_sb_out[l], w_conv_out[l], w_out[l], ln2_g[l], ln2_b[l])
        x = _ffn(x, mod, ffn2_w_gu[l], ffn2_w_down[l], ln3_g[l], ln3_b[l], 2)
    return x
```

```python
import functools
import math

import jax
import jax.numpy as jnp
from jax import lax
from jax.experimental import pallas as pl
from jax.experimental.pallas import tpu as pltpu

F32 = jnp.float32
BF16 = jnp.bfloat16

D_MODEL = 1024
SB_HEADS = 16
SB_HEAD_DIM = 64
D_FF = 2816
CONV_WIDTH = 31
N_MOD_ROWS = 9
MACARON_WEIGHT = 0.5
DEEPNORM_ALPHA = 2.0 ** 0.25
LN_EPS = 1e-5
LOG2_E = 1.0 / math.log(2.0)

ATT_BLK = 256
ATT_TOTAL_ROWS = 16
HEAD_PAIR = 2 * SB_HEAD_DIM
ATT_DIAG_PER_ITER = 4
ATT_PAIRS_PER_ITER = 7
CONV_HALO = 32
LANES = 128
CONV_ROWS = 64

FFN_TM = 512
FFN_FW = 256
INPROJ_TM = 512
INPROJ_GLU_CHUNK = 256
MIX_TM = 512
VMEM_LIMIT = 56 * 1024 * 1024

_NT = (((1,), (1,)), ((), ()))
_TN = (((0,), (0,)), ((), ()))


def _layer_norm(r, g, b):
    mu = jnp.mean(r, axis=-1, keepdims=True)
    d = r - mu
    var = jnp.mean(d * d, axis=-1, keepdims=True)
    return d * lax.rsqrt(var + LN_EPS) * g + b


def _const_spec(shape):
    nd = len(shape)
    return pl.BlockSpec(shape, lambda *_: (0,) * nd, pipeline_mode=pl.Buffered(1))


def _mod_kernel(c_ref, w_ref, b_ref, o_ref):
    c = c_ref[...]
    s = c * jax.nn.sigmoid(c)
    o_ref[...] = jnp.dot(s, w_ref[...], preferred_element_type=F32,
                         precision=lax.Precision.HIGHEST) + b_ref[...]


def _modulation(c, w_ada, b_ada):
    bsz, d = c.shape
    n = w_ada.shape[1]
    tn = 1024
    return pl.pallas_call(
        _mod_kernel,
        grid=(n // tn,),
        in_specs=[pl.BlockSpec((bsz, d), lambda j: (0, 0)),
                  pl.BlockSpec((d, tn), lambda j: (0, j)),
                  pl.BlockSpec((1, tn), lambda j: (0, j))],
        out_specs=pl.BlockSpec((bsz, tn), lambda j: (0, j)),
        out_shape=jax.ShapeDtypeStruct((bsz, n), F32),
        compiler_params=pltpu.CompilerParams(dimension_semantics=("arbitrary",)),
        name="mod",
    )(c, w_ada, b_ada.reshape(1, n))


def _ffn_kernel(x_ref, mod_ref, wgu_ref, wd_ref, lng_ref, lnb_ref, o_ref, h_ref, *, sub):
    x = x_ref[...]
    sh = mod_ref[3 * sub:3 * sub + 1, :]
    sc = mod_ref[3 * sub + 1:3 * sub + 2, :]
    gate = mod_ref[3 * sub + 2:3 * sub + 3, :]
    u = (x * (1.0 + sc) + sh).astype(BF16)
    for c in range(D_FF // FFN_FW):
        lo = c * FFN_FW
        a = jnp.dot(u, wgu_ref[:, lo:lo + FFN_FW], preferred_element_type=F32)
        g = jnp.dot(u, wgu_ref[:, D_FF + lo:D_FF + lo + FFN_FW], preferred_element_type=F32)
        h_ref[:, lo:lo + FFN_FW] = (a * jax.nn.sigmoid(a) * g).astype(BF16)
    y = jnp.dot(h_ref[...], wd_ref[...], preferred_element_type=F32)
    r = DEEPNORM_ALPHA * x + gate * (MACARON_WEIGHT * y)
    o_ref[...] = _layer_norm(r, lng_ref[...], lnb_ref[...])


def _ffn(x, mod, w_gu, w_down, ln_g, ln_b, sub):
    bsz, seq, d = x.shape
    tm = FFN_TM
    return pl.pallas_call(
        functools.partial(_ffn_kernel, sub=sub),
        grid=(bsz, seq // tm),
        in_specs=[pl.BlockSpec((None, tm, d), lambda b, t: (b, t, 0)),
                  pl.BlockSpec((None, N_MOD_ROWS, d), lambda b, t: (b, 0, 0)),
                  _const_spec((d, 2 * D_FF)),
                  _const_spec((D_FF, d)),
                  _const_spec((1, d)),
                  _const_spec((1, d))],
        out_specs=pl.BlockSpec((None, tm, d), lambda b, t: (b, t, 0)),
        out_shape=jax.ShapeDtypeStruct((bsz, seq, d), F32),
        scratch_shapes=[pltpu.VMEM((tm, D_FF), BF16)],
        compiler_params=pltpu.CompilerParams(
            dimension_semantics=("arbitrary", "arbitrary"), vmem_limit_bytes=VMEM_LIMIT),
        name=f"ffn{sub}",
    )(x, mod, w_gu.astype(BF16), w_down.astype(BF16), ln_g.reshape(1, d), ln_b.reshape(1, d))


def _inproj_kernel(x_ref, mod_ref, wqt_ref, wk_ref, wvt_ref, wr_ref, cw_ref, cb_ref, clg_ref, clb_ref,
                   qt_ref, k_ref, vt_ref, act_ref, ga_ref, gb_ref, hext_ref, conv_ref):
    tm = x_ref.shape[0]
    n_slabs = D_MODEL // LANES
    first = pl.program_id(1) == 0

    @pl.when(first)
    def _zero_left_context():
        for s in range(n_slabs):
            hext_ref[s, pl.ds(0, CONV_HALO, stride=2), :] = jnp.zeros((CONV_HALO, LANES), F32)

    @pl.when(jnp.logical_not(first))
    def _carry_left_context():
        for s in range(n_slabs):
            hext_ref[s, pl.ds(0, CONV_HALO, stride=2), :] = hext_ref[s, pl.ds(2 * tm, CONV_HALO, stride=2), :]

    x = x_ref[...]
    sh = mod_ref[3:4, :]
    sc = mod_ref[4:5, :]
    u = (x * (1.0 + sc) + sh).astype(BF16)
    d = D_MODEL
    chunk = INPROJ_GLU_CHUNK
    n_chunks = d // chunk
    slabs_per_chunk = chunk // LANES
    n_tok = tm // ATT_BLK

    def glu_chunk(c):
        a = jnp.dot(u, wr_ref[:, c * chunk:(c + 1) * chunk], preferred_element_type=F32)
        b = jnp.dot(u, wr_ref[:, d + c * chunk:d + (c + 1) * chunk], preferred_element_type=F32)
        h = a * jax.nn.sigmoid(b)
        for s in range(slabs_per_chunk):
            hext_ref[c * slabs_per_chunk + s, pl.ds(2 * CONV_HALO, tm, stride=2), :] = (
                h[:, s * LANES:(s + 1) * LANES])

    def conv_slab(slab):
        base = CONV_HALO - (CONV_WIDTH - 1)
        lanes = slice(slab * LANES, (slab + 1) * LANES)
        taps = [cw_ref[j:j + 1, lanes] for j in range(CONV_WIDTH)]
        bias = cb_ref[:, lanes]
        for r0 in range(0, tm, CONV_ROWS):
            conv = hext_ref[slab, pl.ds(2 * (base + r0), CONV_ROWS, stride=2), :] * taps[0] + bias
            for j in range(1, CONV_WIDTH):
                conv = conv + hext_ref[slab, pl.ds(2 * (base + r0 + j), CONV_ROWS, stride=2), :] * taps[j]
            conv_ref[r0:r0 + CONV_ROWS, lanes] = conv

    def column_chunk(w_ref, w_col, out_ref, gate, c):
        y = jnp.dot(u, w_ref[:, w_col + c * chunk:w_col + (c + 1) * chunk], preferred_element_type=F32)
        out_ref[:, c * chunk:(c + 1) * chunk] = (jax.nn.sigmoid(y) if gate else y).astype(BF16)

    def token_chunk(wt_ref, out_ref, t):
        rows = slice(t * ATT_BLK, (t + 1) * ATT_BLK)
        out_ref[t] = lax.dot_general(wt_ref[...], u[rows, :], _NT,
                                     preferred_element_type=F32).astype(BF16)

    wide = ([functools.partial(token_chunk, wqt_ref, qt_ref, t) for t in range(n_tok)]
            + [functools.partial(token_chunk, wvt_ref, vt_ref, t) for t in range(n_tok)])
    narrow = ([functools.partial(column_chunk, wk_ref, 0, k_ref, False, c) for c in range(n_chunks)]
              + [functools.partial(column_chunk, wr_ref, 2 * d, ga_ref, True, c) for c in range(n_chunks)]
              + [functools.partial(column_chunk, wr_ref, 3 * d, gb_ref, True, c) for c in range(n_chunks)])
    n_slabs_total = n_chunks * slabs_per_chunk
    assert len(wide) * 2 == n_slabs_total and len(narrow) == len(wide) + n_slabs_total
    narrow_it = iter(narrow)
    projections = [[wide[slab // 2], next(narrow_it)] if slab % 2 == 0
                   else [next(narrow_it), next(narrow_it)] for slab in range(n_slabs_total)]

    glu_chunk(0)
    glu_chunk(1)
    for slab in range(n_slabs_total):
        c, s = divmod(slab, slabs_per_chunk)
        if s == 0 and c + 2 < n_chunks:
            glu_chunk(c + 2)
        conv_slab(slab)
        if slab == n_slabs_total - 1:
            cn = _layer_norm(conv_ref[...], clg_ref[...], clb_ref[...])
            act_ref[...] = (cn * jax.nn.sigmoid(cn)).astype(BF16)
        for proj in projections[slab]:
            proj()


def _inproj(x, mod, w_in, conv_w, conv_b, conv_ln_g, conv_ln_b):
    bsz, seq, d = x.shape
    tm = INPROJ_TM
    nb = tm // ATT_BLK
    sbw = SB_HEADS * SB_HEAD_DIM
    scale = LOG2_E / math.sqrt(SB_HEAD_DIM)
    wqt = (w_in[:, 0:sbw] * scale).T.astype(BF16)
    wk = w_in[:, sbw:2 * sbw].astype(BF16)
    wvt = w_in[:, 2 * sbw:3 * sbw].T.astype(BF16)
    wr = w_in[:, 3 * sbw:].astype(BF16)
    tok = pl.BlockSpec((None, tm, d), lambda b, t: (b, t, 0))
    blk_t = pl.BlockSpec((None, nb, sbw, ATT_BLK), lambda b, t: (b, t, 0, 0))
    vec = _const_spec((1, d))
    return pl.pallas_call(
        _inproj_kernel,
        grid=(bsz, seq // tm),
        in_specs=[tok,
                  pl.BlockSpec((None, N_MOD_ROWS, d), lambda b, t: (b, 0, 0)),
                  _const_spec((sbw, d)), _const_spec((d, sbw)), _const_spec((sbw, d)),
                  _const_spec((d, 4 * d)),
                  _const_spec((CONV_WIDTH, d)), vec, vec, vec],
        out_specs=[blk_t, tok, blk_t, tok, tok, tok],
        out_shape=[jax.ShapeDtypeStruct((bsz, seq // ATT_BLK, sbw, ATT_BLK), BF16),
                   jax.ShapeDtypeStruct((bsz, seq, sbw), BF16),
                   jax.ShapeDtypeStruct((bsz, seq // ATT_BLK, sbw, ATT_BLK), BF16),
                   jax.ShapeDtypeStruct((bsz, seq, d), BF16),
                   jax.ShapeDtypeStruct((bsz, seq, d), BF16),
                   jax.ShapeDtypeStruct((bsz, seq, d), BF16)],
        scratch_shapes=[pltpu.VMEM((d // LANES, 2 * (CONV_HALO + tm), LANES), F32),
                        pltpu.VMEM((tm, d), F32)],
        compiler_params=pltpu.CompilerParams(
            dimension_semantics=("arbitrary", "arbitrary"), vmem_limit_bytes=VMEM_LIMIT),
        name="inproj",
    )(x, mod, wqt, wk, wvt, wr, conv_w, conv_b.reshape(1, d), conv_ln_g.reshape(1, d),
      conv_ln_b.reshape(1, d))


def _attn_kernel(qi_ref, kj_ref, qt_ref, k_ref, vt_ref, ut_ref, o_ref, qm_ref, acc_ref, later_ref):
    blk = ATT_BLK
    n_blk = qt_ref.shape[0]
    n_off = qi_ref.shape[0]
    dim_row = lax.broadcasted_iota(jnp.int32, (HEAD_PAIR, blk), 0)
    key_pos = lax.broadcasted_iota(jnp.int32, (blk, blk), 0)
    qry_pos = lax.broadcasted_iota(jnp.int32, (blk, blk), 1)
    causal = key_pos < qry_pos
    half = blk // 2
    ut_tri = ut_ref[0:half, 0:half]
    ut_tail = ut_ref[half:half + ATT_TOTAL_ROWS, :]

    def head_rows(head):
        return slice(head * SB_HEAD_DIM, (head + 1) * SB_HEAD_DIM)

    def blocks(work, masked):
        kjs = [k_ref[pl.ds(pl.multiple_of(j * blk, blk), blk), :] for _, _, j in work]
        zts = [jnp.dot(kj, qh, preferred_element_type=F32)
               for kj, (qh, _, _) in zip(kjs, work)]
        log_keeps = []
        for zt in zts:
            neg = jnp.minimum(zt, 0.0)
            npos = neg - zt
            sp = jnp.log(1.0 + jnp.exp2(neg + npos)) * LOG2_E
            log_keep = npos - sp
            if masked:
                log_keep = jnp.where(causal, log_keep, 0.0)
            log_keeps.append(log_keep.astype(BF16))
        halves = []
        for lk, kj, (qh, _, _) in zip(log_keeps, kjs, work):
            pair = []
            for rows in (slice(0, half), slice(half, blk)):
                lhs = jnp.concatenate(
                    [jnp.concatenate([ut_tri, kj[rows]], axis=1), ut_tail], axis=0)
                rhs = jnp.concatenate([lk[rows], qh], axis=0)
                pair.append(jnp.dot(lhs, rhs, preferred_element_type=F32))
            halves.append(pair)
        ws, totals = [], []
        for ex_lo, ex_hi in halves:
            total_hi = ex_hi[half:half + 8]
            behind = jnp.concatenate([total_hi] * (half // 8), axis=0)
            w = jnp.exp2(jnp.concatenate([ex_lo[0:half] + behind, ex_hi[0:half]], axis=0))
            if masked:
                w = jnp.where(causal, w, 0.0)
            ws.append(w.astype(BF16))
            totals.append(total_hi + ex_lo[half:half + 8])
        pvs = [jnp.dot(vt_ref[j, head_rows(head), :], w, preferred_element_type=F32)
               for (_, head, j), w in zip(work, ws)]
        return list(zip(pvs, totals))

    def diagonal_step(it, carry_unused):
        work = []
        for u in range(ATT_DIAG_PER_ITER):
            i = it + u * (n_blk // ATT_DIAG_PER_ITER)
            qt = qt_ref[i]
            for head in range(2):
                own = (dim_row >= SB_HEAD_DIM) if head else (dim_row < SB_HEAD_DIM)
                work.append((jnp.where(own, qt, jnp.zeros_like(qt)), head, i))
        for (qh, head, i), (pv, total) in zip(work, blocks(work, True)):
            qm_ref[i, head] = qh
            acc_ref[i, head_rows(head), :] = pv
            later_ref[i, head] = total
        return carry_unused

    def off_diagonal_step(it, carry_unused):
        work, dest = [], []
        for u in range(ATT_PAIRS_PER_ITER):
            n = it * ATT_PAIRS_PER_ITER + u
            i = qi_ref[n]
            j = kj_ref[n]
            for head in range(2):
                work.append((qm_ref[i, head], head, j))
                dest.append(i)
        for (_, head, _), i, (pv, total) in zip(work, dest, blocks(work, False)):
            later = later_ref[i, head]
            scale = jnp.concatenate([jnp.exp2(later)] * (SB_HEAD_DIM // 8), axis=0)
            acc_ref[i, head_rows(head), :] = acc_ref[i, head_rows(head), :] + scale * pv
            later_ref[i, head] = later + total
        return carry_unused

    lax.fori_loop(0, n_blk // ATT_DIAG_PER_ITER, diagonal_step, 0)
    lax.fori_loop(0, n_off // ATT_PAIRS_PER_ITER, off_diagonal_step, 0)
    o_ref[...] = acc_ref[...].astype(BF16)


def _attention(qt, k, vt):
    bsz, n_blk, sbw, blk = qt.shape
    seq = k.shape[1]
    half = blk // 2
    ut = jnp.concatenate(
        [jnp.triu(jnp.ones((half, half), F32)), jnp.ones((ATT_TOTAL_ROWS, half), F32)], axis=0)
    ut = jnp.concatenate([ut, jnp.zeros_like(ut)], axis=1).astype(BF16)
    pairs = [(i, j) for i in range(1, n_blk) for j in range(i - 1, -1, -1)]
    assert n_blk % ATT_DIAG_PER_ITER == 0 and len(pairs) % ATT_PAIRS_PER_ITER == 0
    qi = jnp.asarray([p[0] for p in pairs], jnp.int32)
    kj = jnp.asarray([p[1] for p in pairs], jnp.int32)
    n_pairs = sbw // HEAD_PAIR
    blk_t = pl.BlockSpec((None, n_blk, HEAD_PAIR, blk), lambda b, p, *_: (b, 0, p, 0))
    tok = pl.BlockSpec((None, seq, HEAD_PAIR), lambda b, p, *_: (b, 0, p))
    return pl.pallas_call(
        _attn_kernel,
        grid_spec=pltpu.PrefetchScalarGridSpec(
            num_scalar_prefetch=2,
            grid=(bsz, n_pairs),
            in_specs=[blk_t, tok, blk_t, _const_spec((half + ATT_TOTAL_ROWS, blk))],
            out_specs=blk_t,
            scratch_shapes=[pltpu.VMEM((n_blk, 2, HEAD_PAIR, blk), BF16),
                            pltpu.VMEM((n_blk, HEAD_PAIR, blk), F32),
                            pltpu.VMEM((n_blk, 2, 8, blk), F32)]),
        out_shape=jax.ShapeDtypeStruct((bsz, n_blk, sbw, blk), BF16),
        compiler_params=pltpu.CompilerParams(
            dimension_semantics=("arbitrary", "arbitrary"), vmem_limit_bytes=VMEM_LIMIT),
        name="attn",
    )(qi, kj, qt, k, vt, ut)


def _mix_kernel(x_ref, mod_ref, attn_ref, act_ref, ga_ref, gb_ref,
                wsb_ref, wco_ref, wo_ref, lng_ref, lnb_ref, o_ref):
    y_conv = jnp.dot(act_ref[...], wco_ref[...], preferred_element_type=F32)
    y_sb = jnp.concatenate(
        [lax.dot_general(attn_ref[t], wsb_ref[...], _TN, preferred_element_type=F32)
         for t in range(attn_ref.shape[0])], axis=0)
    merged = ga_ref[...].astype(F32) * y_sb + gb_ref[...].astype(F32) * y_conv
    o = jnp.dot(merged.astype(BF16), wo_ref[...], preferred_element_type=F32)
    x = x_ref[...]
    gate = mod_ref[5:6, :]
    r = DEEPNORM_ALPHA * x + gate * o
    o_ref[...] = _layer_norm(r, lng_ref[...], lnb_ref[...])


def _mix(x, mod, attn, act, ga, gb, w_sb_out, w_conv_out, w_out, ln_g, ln_b):
    bsz, seq, d = x.shape
    tm = MIX_TM
    tok = pl.BlockSpec((None, tm, d), lambda b, t: (b, t, 0))
    attn_t = pl.BlockSpec((None, tm // ATT_BLK, attn.shape[2], ATT_BLK), lambda b, t: (b, t, 0, 0))
    vec = _const_spec((1, d))
    return pl.pallas_call(
        _mix_kernel,
        grid=(bsz, seq // tm),
        in_specs=[tok,
                  pl.BlockSpec((None, N_MOD_ROWS, d), lambda b, t: (b, 0, 0)),
                  attn_t, tok, tok, tok,
                  _const_spec((d, d)), _const_spec((d, d)), _const_spec((d, d)), vec, vec],
        out_specs=tok,
        out_shape=jax.ShapeDtypeStruct((bsz, seq, d), F32),
        compiler_params=pltpu.CompilerParams(
            dimension_semantics=("arbitrary", "arbitrary"), vmem_limit_bytes=VMEM_LIMIT),
        name="mix",
    )(x, mod, attn, act, ga, gb, w_sb_out.astype(BF16), w_conv_out.astype(BF16), w_out.astype(BF16),
      ln_g.reshape(1, d), ln_b.reshape(1, d))


def kernel(x, c, w_ada, b_ada, ffn1_w_gu, ffn1_w_down, ln1_g, ln1_b, w_in, w_sb_out, conv_w, conv_b,
           conv_ln_g, conv_ln_b, w_conv_out, w_out, ln2_g, ln2_b, ffn2_w_gu, ffn2_w_down, ln3_g, ln3_b):
    bsz, seq, d = x.shape
    depth = w_ada.shape[0]
    for l in range(depth):
        mod = _modulation(c, w_ada[l], b_ada[l]).reshape(bsz, N_MOD_ROWS, d)
        x = _ffn(x, mod, ffn1_w_gu[l], ffn1_w_down[l], ln1_g[l], ln1_b[l], 0)
        qt, k, vt, act, ga, gb = _inproj(x, mod, w_in[l], conv_w[l], conv_b[l], conv_ln_g[l], conv_ln_b[l])
        attn = _attention(qt, k, vt)
        x = _mix(x, mod, attn, act, ga, gb, w_sb_out[l], w_conv_out[l], w_out[l], ln2_g[l], ln2_b[l])
        x = _ffn(x, mod, ffn2_w_gu[l], ffn2_w_down[l], ln3_g[l], ln3_b[l], 2)
    return x
```

```python
import functools
import math

import jax
import jax.numpy as jnp
from jax import lax
from jax.experimental import pallas as pl
from jax.experimental.pallas import tpu as pltpu

F32 = jnp.float32
BF16 = jnp.bfloat16

D_MODEL = 1024
SB_HEADS = 16
SB_HEAD_DIM = 64
D_FF = 2816
CONV_WIDTH = 31
N_MOD_ROWS = 9
MACARON_WEIGHT = 0.5
DEEPNORM_ALPHA = 2.0 ** 0.25
LN_EPS = 1e-5
LOG2_E = 1.0 / math.log(2.0)

ATT_BLK = 256
ATT_TOTAL_ROWS = 16
HEAD_PAIR = 2 * SB_HEAD_DIM
ATT_DIAG_PER_ITER = 4
ATT_PAIRS_PER_ITER = 7
CONV_HALO = 32
LANES = 128
CONV_ROWS = 64

FFN_TM = 512
FFN_FW = 256
INPROJ_TM = 512
INPROJ_GLU_CHUNK = 256
MIX_TM = 512
VMEM_LIMIT = 56 * 1024 * 1024

_NT = (((1,), (1,)), ((), ()))
_TN = (((0,), (0,)), ((), ()))


def _layer_norm(r, g, b):
    mu = jnp.mean(r, axis=-1, keepdims=True)
    d = r - mu
    var = jnp.mean(d * d, axis=-1, keepdims=True)
    return d * lax.rsqrt(var + LN_EPS) * g + b


def _const_spec(shape):
    nd = len(shape)
    return pl.BlockSpec(shape, lambda *_: (0,) * nd, pipeline_mode=pl.Buffered(1))


def _mod_kernel(c_ref, w_ref, b_ref, o_ref):
    c = c_ref[...]
    s = c * jax.nn.sigmoid(c)
    o_ref[...] = jnp.dot(s, w_ref[...], preferred_element_type=F32,
                         precision=lax.Precision.HIGHEST) + b_ref[...]


def _modulation(c, w_ada, b_ada):
    bsz, d = c.shape
    n = w_ada.shape[1]
    tn = 1024
    return pl.pallas_call(
        _mod_kernel,
        grid=(n // tn,),
        in_specs=[pl.BlockSpec((bsz, d), lambda j: (0, 0)),
                  pl.BlockSpec((d, tn), lambda j: (0, j)),
                  pl.BlockSpec((1, tn), lambda j: (0, j))],
        out_specs=pl.BlockSpec((bsz, tn), lambda j: (0, j)),
        out_shape=jax.ShapeDtypeStruct((bsz, n), F32),
        compiler_params=pltpu.CompilerParams(dimension_semantics=("arbitrary",)),
        name="mod",
    )(c, w_ada, b_ada.reshape(1, n))


def _ffn_kernel(x_ref, mod_ref, wgu_ref, wd_ref, lng_ref, lnb_ref, o_ref, h_ref, *, sub):
    x = x_ref[...]
    sh = mod_ref[3 * sub:3 * sub + 1, :]
    sc = mod_ref[3 * sub + 1:3 * sub + 2, :]
    gate = mod_ref[3 * sub + 2:3 * sub + 3, :]
    u = (x * (1.0 + sc) + sh).astype(BF16)
    for c in range(D_FF // FFN_FW):
        lo = c * FFN_FW
        a = jnp.dot(u, wgu_ref[:, lo:lo + FFN_FW], preferred_element_type=F32)
        g = jnp.dot(u, wgu_ref[:, D_FF + lo:D_FF + lo + FFN_FW], preferred_element_type=F32)
        h_ref[:, lo:lo + FFN_FW] = (a * jax.nn.sigmoid(a) * g).astype(BF16)
    y = jnp.dot(h_ref[...], wd_ref[...], preferred_element_type=F32)
    r = DEEPNORM_ALPHA * x + gate * (MACARON_WEIGHT * y)
    o_ref[...] = _layer_norm(r, lng_ref[...], lnb_ref[...])


def _ffn(x, mod, w_gu, w_down, ln_g, ln_b, sub):
    bsz, seq, d = x.shape
    tm = FFN_TM
    return pl.pallas_call(
        functools.partial(_ffn_kernel, sub=sub),
        grid=(bsz, seq // tm),
        in_specs=[pl.BlockSpec((None, tm, d), lambda b, t: (b, t, 0)),
                  pl.BlockSpec((None, N_MOD_ROWS, d), lambda b, t: (b, 0, 0)),
                  _const_spec((d, 2 * D_FF)),
                  _const_spec((D_FF, d)),
                  _const_spec((1, d)),
                  _const_spec((1, d))],
        out_specs=pl.BlockSpec((None, tm, d), lambda b, t: (b, t, 0)),
        out_shape=jax.ShapeDtypeStruct((bsz, seq, d), F32),
        scratch_shapes=[pltpu.VMEM((tm, D_FF), BF16)],
        compiler_params=pltpu.CompilerParams(
            dimension_semantics=("arbitrary", "arbitrary"), vmem_limit_bytes=VMEM_LIMIT),
        name=f"ffn{sub}",
    )(x, mod, w_gu.astype(BF16), w_down.astype(BF16), ln_g.reshape(1, d), ln_b.reshape(1, d))


def _inproj_kernel(x_ref, mod_ref, wqt_ref, wk_ref, wvt_ref, wr_ref, cw_ref, cb_ref, clg_ref, clb_ref,
                   qt_ref, k_ref, vt_ref, act_ref, ga_ref, gb_ref, hext_ref, conv_ref):
    tm = x_ref.shape[0]
    n_slabs = D_MODEL // LANES
    first = pl.program_id(1) == 0

    @pl.when(first)
    def _zero_left_context():
        for s in range(n_slabs):
            hext_ref[s, pl.ds(0, CONV_HALO, stride=2), :] = jnp.zeros((CONV_HALO, LANES), F32)

    @pl.when(jnp.logical_not(first))
    def _carry_left_context():
        for s in range(n_slabs):
            hext_ref[s, pl.ds(0, CONV_HALO, stride=2), :] = hext_ref[s, pl.ds(2 * tm, CONV_HALO, stride=2), :]

    x = x_ref[...]
    sh = mod_ref[3:4, :]
    sc = mod_ref[4:5, :]
    u = (x * (1.0 + sc) + sh).astype(BF16)
    d = D_MODEL
    chunk = INPROJ_GLU_CHUNK
    n_chunks = d // chunk
    slabs_per_chunk = chunk // LANES
    n_tok = tm // ATT_BLK

    def glu_chunk(c):
        a = jnp.dot(u, wr_ref[:, c * chunk:(c + 1) * chunk], preferred_element_type=F32)
        b = jnp.dot(u, wr_ref[:, d + c * chunk:d + (c + 1) * chunk], preferred_element_type=F32)
        h = a * jax.nn.sigmoid(b)
        for s in range(slabs_per_chunk):
            hext_ref[c * slabs_per_chunk + s, pl.ds(2 * CONV_HALO, tm, stride=2), :] = (
                h[:, s * LANES:(s + 1) * LANES])

    def conv_slab(slab):
        base = CONV_HALO - (CONV_WIDTH - 1)
        lanes = slice(slab * LANES, (slab + 1) * LANES)
        taps = [cw_ref[j:j + 1, lanes] for j in range(CONV_WIDTH)]
        bias = cb_ref[:, lanes]
        for r0 in range(0, tm, CONV_ROWS):
            conv = hext_ref[slab, pl.ds(2 * (base + r0), CONV_ROWS, stride=2), :] * taps[0] + bias
            for j in range(1, CONV_WIDTH):
                conv = conv + hext_ref[slab, pl.ds(2 * (base + r0 + j), CONV_ROWS, stride=2), :] * taps[j]
            conv_ref[r0:r0 + CONV_ROWS, lanes] = conv

    def column_chunk(w_ref, w_col, out_ref, gate, c):
        y = jnp.dot(u, w_ref[:, w_col + c * chunk:w_col + (c + 1) * chunk], preferred_element_type=F32)
        out_ref[:, c * chunk:(c + 1) * chunk] = (jax.nn.sigmoid(y) if gate else y).astype(BF16)

    def token_chunk(wt_ref, out_ref, t):
        rows = slice(t * ATT_BLK, (t + 1) * ATT_BLK)
        out_ref[t] = lax.dot_general(wt_ref[...], u[rows, :], _NT,
                                     preferred_element_type=F32).astype(BF16)

    wide = ([functools.partial(token_chunk, wqt_ref, qt_ref, t) for t in range(n_tok)]
            + [functools.partial(token_chunk, wvt_ref, vt_ref, t) for t in range(n_tok)])
    narrow = ([functools.partial(column_chunk, wk_ref, 0, k_ref, False, c) for c in range(n_chunks)]
              + [functools.partial(column_chunk, wr_ref, 2 * d, ga_ref, True, c) for c in range(n_chunks)]
              + [functools.partial(column_chunk, wr_ref, 3 * d, gb_ref, True, c) for c in range(n_chunks)])
    n_slabs_total = n_chunks * slabs_per_chunk
    assert len(wide) * 2 == n_slabs_total and len(narrow) == len(wide) + n_slabs_total
    narrow_it = iter(narrow)
    projections = [[wide[slab // 2], next(narrow_it)] if slab % 2 == 0
                   else [next(narrow_it), next(narrow_it)] for slab in range(n_slabs_total)]

    glu_chunk(0)
    glu_chunk(1)
    for slab in range(n_slabs_total):
        c, s = divmod(slab, slabs_per_chunk)
        if s == 0 and c + 2 < n_chunks:
            glu_chunk(c + 2)
        conv_slab(slab)
        if slab == n_slabs_total - 1:
            cn = _layer_norm(conv_ref[...], clg_ref[...], clb_ref[...])
            act_ref[...] = (cn * jax.nn.sigmoid(cn)).astype(BF16)
        for proj in projections[slab]:
            proj()


def _inproj(x, mod, w_in, conv_w, conv_b, conv_ln_g, conv_ln_b):
    bsz, seq, d = x.shape
    tm = INPROJ_TM
    nb = tm // ATT_BLK
    sbw = SB_HEADS * SB_HEAD_DIM
    scale = LOG2_E / math.sqrt(SB_HEAD_DIM)
    wqt = (w_in[:, 0:sbw] * scale).T.astype(BF16)
    wk = w_in[:, sbw:2 * sbw].astype(BF16)
    wvt = w_in[:, 2 * sbw:3 * sbw].T.astype(BF16)
    wr = w_in[:, 3 * sbw:].astype(BF16)
    tok = pl.BlockSpec((None, tm, d), lambda b, t: (b, t, 0))
    blk_t = pl.BlockSpec((None, nb, sbw, ATT_BLK), lambda b, t: (b, t, 0, 0))
    vec = _const_spec((1, d))
    return pl.pallas_call(
        _inproj_kernel,
        grid=(bsz, seq // tm),
        in_specs=[tok,
                  pl.BlockSpec((None, N_MOD_ROWS, d), lambda b, t: (b, 0, 0)),
                  _const_spec((sbw, d)), _const_spec((d, sbw)), _const_spec((sbw, d)),
                  _const_spec((d, 4 * d)),
                  _const_spec((CONV_WIDTH, d)), vec, vec, vec],
        out_specs=[blk_t, tok, blk_t, tok, tok, tok],
        out_shape=[jax.ShapeDtypeStruct((bsz, seq // ATT_BLK, sbw, ATT_BLK), BF16),
                   jax.ShapeDtypeStruct((bsz, seq, sbw), BF16),
                   jax.ShapeDtypeStruct((bsz, seq // ATT_BLK, sbw, ATT_BLK), BF16),
                   jax.ShapeDtypeStruct((bsz, seq, d), BF16),
                   jax.ShapeDtypeStruct((bsz, seq, d), BF16),
                   jax.ShapeDtypeStruct((bsz, seq, d), BF16)],
        scratch_shapes=[pltpu.VMEM((d // LANES, 2 * (CONV_HALO + tm), LANES), F32),
                        pltpu.VMEM((tm, d), F32)],
        compiler_params=pltpu.CompilerParams(
            dimension_semantics=("arbitrary", "arbitrary"), vmem_limit_bytes=VMEM_LIMIT),
        name="inproj",
    )(x, mod, wqt, wk, wvt, wr, conv_w, conv_b.reshape(1, d), conv_ln_g.reshape(1, d),
      conv_ln_b.reshape(1, d))


def _attn_kernel(qi_ref, kj_ref, qt_ref, k_ref, vt_ref, ut_ref, o_ref, qm_ref, acc_ref, later_ref):
    blk = ATT_BLK
    n_blk = qt_ref.shape[0]
    n_off = qi_ref.shape[0]
    dim_row = lax.broadcasted_iota(jnp.int32, (HEAD_PAIR, blk), 0)
    key_pos = lax.broadcasted_iota(jnp.int32, (blk, blk), 0)
    qry_pos = lax.broadcasted_iota(jnp.int32, (blk, blk), 1)
    causal = key_pos < qry_pos
    half = blk // 2
    ut_tri = ut_ref[0:half, 0:half]
    ut_tail = ut_ref[half:half + ATT_TOTAL_ROWS, :]

    def head_rows(head):
        return slice(head * SB_HEAD_DIM, (head + 1) * SB_HEAD_DIM)

    def blocks(work, masked):
        kjs = [k_ref[pl.ds(pl.multiple_of(j * blk, blk), blk), :] for _, _, j in work]
        zts = [jnp.dot(kj, qh, preferred_element_type=F32)
               for kj, (qh, _, _) in zip(kjs, work)]
        log_keeps = []
        for zt in zts:
            neg = jnp.minimum(zt, 0.0)
            npos = neg - zt
            sp = jnp.log(1.0 + jnp.exp2(neg + npos)) * LOG2_E
            log_keep = npos - sp
            if masked:
                log_keep = jnp.where(causal, log_keep, 0.0)
            log_keeps.append(log_keep.astype(BF16))
        halves = []
        for lk, kj, (qh, _, _) in zip(log_keeps, kjs, work):
            pair = []
            for rows in (slice(0, half), slice(half, blk)):
                lhs = jnp.concatenate(
                    [jnp.concatenate([ut_tri, kj[rows]], axis=1), ut_tail], axis=0)
                rhs = jnp.concatenate([lk[rows], qh], axis=0)
                pair.append(jnp.dot(lhs, rhs, preferred_element_type=F32))
            halves.append(pair)
        ws, totals = [], []
        for ex_lo, ex_hi in halves:
            total_hi = ex_hi[half:half + 8]
            behind = jnp.concatenate([total_hi] * (half // 8), axis=0)
            w = jnp.exp2(jnp.concatenate([ex_lo[0:half] + behind, ex_hi[0:half]], axis=0))
            if masked:
                w = jnp.where(causal, w, 0.0)
            ws.append(w.astype(BF16))
            totals.append(total_hi + ex_lo[half:half + 8])
        pvs = [jnp.dot(vt_ref[j, head_rows(head), :], w, preferred_element_type=F32)
               for (_, head, j), w in zip(work, ws)]
        return list(zip(pvs, totals))

    def diagonal_step(it, carry_unused):
        work = []
        for u in range(ATT_DIAG_PER_ITER):
            i = it + u * (n_blk // ATT_DIAG_PER_ITER)
            qt = qt_ref[i]
            for head in range(2):
                own = (dim_row >= SB_HEAD_DIM) if head else (dim_row < SB_HEAD_DIM)
                work.append((jnp.where(own, qt, jnp.zeros_like(qt)), head, i))
        for (qh, head, i), (pv, total) in zip(work, blocks(work, True)):
            qm_ref[i, head] = qh
            acc_ref[i, head_rows(head), :] = pv
            later_ref[i, head] = total
        return carry_unused

    def off_diagonal_step(it, carry_unused):
        work, dest = [], []
        for u in range(ATT_PAIRS_PER_ITER):
            n = it * ATT_PAIRS_PER_ITER + u
            i = qi_ref[n]
            j = kj_ref[n]
            for head in range(2):
                work.append((qm_ref[i, head], head, j))
                dest.append(i)
        for (_, head, _), i, (pv, total) in zip(work, dest, blocks(work, False)):
            later = later_ref[i, head]
            scale = jnp.concatenate([jnp.exp2(later)] * (SB_HEAD_DIM // 8), axis=0)
            acc_ref[i, head_rows(head), :] = acc_ref[i, head_rows(head), :] + scale * pv
            later_ref[i, head] = later + total
        return carry_unused

    lax.fori_loop(0, n_blk // ATT_DIAG_PER_ITER, diagonal_step, 0)
    lax.fori_loop(0, n_off // ATT_PAIRS_PER_ITER, off_diagonal_step, 0)
    o_ref[...] = acc_ref[...].astype(BF16)


def _attention(qt, k, vt):
    bsz, n_blk, sbw, blk = qt.shape
    seq = k.shape[1]
    half = blk // 2
    ut = jnp.concatenate(
        [jnp.triu(jnp.ones((half, half), F32)), jnp.ones((ATT_TOTAL_ROWS, half), F32)], axis=0)
    ut = jnp.concatenate([ut, jnp.zeros_like(ut)], axis=1).astype(BF16)
    pairs = [(i, j) for i in range(1, n_blk) for j in range(i - 1, -1, -1)]
    assert n_blk % ATT_DIAG_PER_ITER == 0 and len(pairs) % ATT_PAIRS_PER_ITER == 0
    qi = jnp.asarray([p[0] for p in pairs], jnp.int32)
    kj = jnp.asarray([p[1] for p in pairs], jnp.int32)
    n_pairs = sbw // HEAD_PAIR
    blk_t = pl.BlockSpec((None, n_blk, HEAD_PAIR, blk), lambda b, p, *_: (b, 0, p, 0))
    tok = pl.BlockSpec((None, seq, HEAD_PAIR), lambda b, p, *_: (b, 0, p))
    return pl.pallas_call(
        _attn_kernel,
        grid_spec=pltpu.PrefetchScalarGridSpec(
            num_scalar_prefetch=2,
            grid=(bsz, n_pairs),
            in_specs=[blk_t, tok, blk_t, _const_spec((half + ATT_TOTAL_ROWS, blk))],
            out_specs=blk_t,
            scratch_shapes=[pltpu.VMEM((n_blk, 2, HEAD_PAIR, blk), BF16),
                            pltpu.VMEM((n_blk, HEAD_PAIR, blk), F32),
                            pltpu.VMEM((n_blk, 2, 8, blk), F32)]),
        out_shape=jax.ShapeDtypeStruct((bsz, n_blk, sbw, blk), BF16),
        compiler_params=pltpu.CompilerParams(
            dimension_semantics=("arbitrary", "arbitrary"), vmem_limit_bytes=VMEM_LIMIT),
        name="attn",
    )(qi, kj, qt, k, vt, ut)


def _mix_kernel(x_ref, mod_ref, attn_ref, act_ref, ga_ref, gb_ref,
                wsb_ref, wco_ref, wo_ref, lng_ref, lnb_ref, o_ref):
    y_conv = jnp.dot(act_ref[...], wco_ref[...], preferred_element_type=F32)
    y_sb = jnp.concatenate(
        [lax.dot_general(attn_ref[t], wsb_ref[...], _TN, preferred_element_type=F32)
         for t in range(attn_ref.shape[0])], axis=0)
    merged = ga_ref[...].astype(F32) * y_sb + gb_ref[...].astype(F32) * y_conv
    o = jnp.dot(merged.astype(BF16), wo_ref[...], preferred_element_type=F32)
    x = x_ref[...]
    gate = mod_ref[5:6, :]
    r = DEEPNORM_ALPHA * x + gate * o
    o_ref[...] = _layer_norm(r, lng_ref[...], lnb_ref[...])


def _mix(x, mod, attn, act, ga, gb, w_sb_out, w_conv_out, w_out, ln_g, ln_b):
    bsz, seq, d = x.shape
    tm = MIX_TM
    tok = pl.BlockSpec((None, tm, d), lambda b, t: (b, t, 0))
    attn_t = pl.BlockSpec((None, tm // ATT_BLK, attn.shape[2], ATT_BLK), lambda b, t: (b, t, 0, 0))
    vec = _const_spec((1, d))
    return pl.pallas_call(
        _mix_kernel,
        grid=(bsz, seq // tm),
        in_specs=[tok,
                  pl.BlockSpec((None, N_MOD_ROWS, d), lambda b, t: (b, 0, 0)),
                  attn_t, tok, tok, tok,
                  _const_spec((d, d)), _const_spec((d, d)), _const_spec((d, d)), vec, vec],
        out_specs=tok,
        out_shape=jax.ShapeDtypeStruct((bsz, seq, d), F32),
        compiler_params=pltpu.CompilerParams(
            dimension_semantics=("arbitrary", "arbitrary"), vmem_limit_bytes=VMEM_LIMIT),
        name="mix",
    )(x, mod, attn, act, ga, gb, w_sb_out.astype(BF16), w_conv_out.astype(BF16), w_out.astype(BF16),
      ln_g.reshape(1, d), ln_b.reshape(1, d))


def _mix_ffn_kernel(x_ref, mod_ref, attn_ref, act_ref, ga_ref, gb_ref, wsb_ref, wco_ref, wo_ref,
                    lng_ref, lnb_ref, wgu_ref, wd_ref, ln3g_ref, ln3b_ref, o_ref, xmid_ref, h_ref):
    _mix_kernel(x_ref, mod_ref, attn_ref, act_ref, ga_ref, gb_ref,
                wsb_ref, wco_ref, wo_ref, lng_ref, lnb_ref, xmid_ref)
    _ffn_kernel(xmid_ref, mod_ref, wgu_ref, wd_ref, ln3g_ref, ln3b_ref, o_ref, h_ref, sub=2)


def _mix_ffn(x, mod, attn, act, ga, gb, w_sb_out, w_conv_out, w_out, ln_g, ln_b,
             w_gu, w_down, ln3_g, ln3_b):
    bsz, seq, d = x.shape
    tm = MIX_TM
    tok = pl.BlockSpec((None, tm, d), lambda b, t: (b, t, 0))
    attn_t = pl.BlockSpec((None, tm // ATT_BLK, attn.shape[2], ATT_BLK), lambda b, t: (b, t, 0, 0))
    vec = _const_spec((1, d))
    return pl.pallas_call(
        _mix_ffn_kernel,
        grid=(bsz, seq // tm),
        in_specs=[tok,
                  pl.BlockSpec((None, N_MOD_ROWS, d), lambda b, t: (b, 0, 0)),
                  attn_t, tok, tok, tok,
                  _const_spec((d, d)), _const_spec((d, d)), _const_spec((d, d)), vec, vec,
                  _const_spec((d, 2 * D_FF)), _const_spec((D_FF, d)), vec, vec],
        out_specs=tok,
        out_shape=jax.ShapeDtypeStruct((bsz, seq, d), F32),
        scratch_shapes=[pltpu.VMEM((tm, d), F32), pltpu.VMEM((tm, D_FF), BF16)],
        compiler_params=pltpu.CompilerParams(
            dimension_semantics=("arbitrary", "arbitrary"), vmem_limit_bytes=VMEM_LIMIT),
        name="mixffn",
    )(x, mod, attn, act, ga, gb, w_sb_out.astype(BF16), w_conv_out.astype(BF16), w_out.astype(BF16),
      ln_g.reshape(1, d), ln_b.reshape(1, d), w_gu.astype(BF16), w_down.astype(BF16),
      ln3_g.reshape(1, d), ln3_b.reshape(1, d))


def kernel(x, c, w_ada, b_ada, ffn1_w_gu, ffn1_w_down, ln1_g, ln1_b, w_in, w_sb_out, conv_w, conv_b,
           conv_ln_g, conv_ln_b, w_conv_out, w_out, ln2_g, ln2_b, ffn2_w_gu, ffn2_w_down, ln3_g, ln3_b):
    bsz, seq, d = x.shape
    depth = w_ada.shape[0]
    for l in range(depth):
        mod = _modulation(c, w_ada[l], b_ada[l]).reshape(bsz, N_MOD_ROWS, d)
        x = _ffn(x, mod, ffn1_w_gu[l], ffn1_w_down[l], ln1_g[l], ln1_b[l], 0)
        qt, k, vt, act, ga, gb = _inproj(x, mod, w_in[l], conv_w[l], conv_b[l], conv_ln_g[l], conv_ln_b[l])
        attn = _attention(qt, k, vt)
        x = _mix_ffn(x, mod, attn, act, ga, gb, w_sb_out[l], w_conv_out[l], w_out[l], ln2_g[l], ln2_b[l],
                     ffn2_w_gu[l], ffn2_w_down[l], ln3_g[l], ln3_b[l])
    return x
```
